```python
import jax, jax.numpy as jnp
from jax import lax
import numpy as np

D_MODEL = 1024
BATCH = 4
SEQ = 8192
DEPTH = 1

D_MIX = D_MODEL
HEAD_DIM = 64
GDN_HEADS = (D_MIX // 2) // HEAD_DIM
GDN_WIDTH = GDN_HEADS * HEAD_DIM
CONV_K = 4
CHUNK = 64
SWA_Q_HEADS = (D_MIX // 2) // HEAD_DIM
SWA_KV_HEADS = 2
SWA_GROUP = SWA_Q_HEADS // SWA_KV_HEADS
SWA_Q_WIDTH = SWA_Q_HEADS * HEAD_DIM
SWA_KV_WIDTH = SWA_KV_HEADS * HEAD_DIM
WINDOW = 128
BLOCK = 128
ROPE_THETA = 10000.0
N_GROUPS = 4
EXPERTS_PER_GROUP = 8
N_EXPERTS = N_GROUPS * EXPERTS_PER_GROUP
TOP_K = 2
D_EXPERT = D_MODEL // 4
EPS = 1e-6
IN_WIDTHS = (GDN_WIDTH, GDN_WIDTH, GDN_WIDTH, GDN_WIDTH, GDN_HEADS, GDN_HEADS,
             SWA_Q_WIDTH, SWA_KV_WIDTH, SWA_KV_WIDTH)
IN_WIDTH = sum(IN_WIDTHS)
MIX_OUT_WIDTH = GDN_WIDTH + SWA_Q_WIDTH

kernel_name = "hybrid_gdn_swa_sink_hmoe_adaln"


def rms_norm(x, gain):
    xf = x.astype(jnp.float32)
    y = xf * lax.rsqrt(jnp.mean(xf * xf, axis=-1, keepdims=True) + EPS)
    return (y * gain.astype(jnp.float32)).astype(x.dtype)


def l2_norm(x):
    return x * lax.rsqrt(jnp.sum(x * x, axis=-1, keepdims=True) + EPS)


def modulate(x, shift, scale):
    return x * (1 + scale[:, None, :]) + shift[:, None, :]


def causal_conv_silu(x, w):
    seq = x.shape[1]
    xp = jnp.pad(x, ((0, 0), (CONV_K - 1, 0), (0, 0)))
    y = xp[:, 0:seq] * w[0]
    for j in range(1, CONV_K):
        y = y + xp[:, j:j + seq] * w[j]
    return jax.nn.silu(y)


def rope(x, positions):
    half = x.shape[-1] // 2
    inv_freq = jnp.power(jnp.float32(ROPE_THETA), -jnp.arange(half, dtype=jnp.float32) / half)
    ang = positions.astype(jnp.float32)[..., None] * inv_freq
    cos = jnp.cos(ang)[:, :, None, :]
    sin = jnp.sin(ang)[:, :, None, :]
    xf = x.astype(jnp.float32)
    x1, x2 = xf[..., :half], xf[..., half:]
    out = jnp.concatenate([x1 * cos - x2 * sin, x1 * sin + x2 * cos], axis=-1)
    return out.astype(x.dtype)


def chunked_gated_delta_rule(q, k, v, g, beta):
    bsz, seq, heads, dh = q.shape
    n_chunks = seq // CHUNK

    def to_chunks(t):
        return t.reshape(bsz, n_chunks, CHUNK, heads, -1).transpose(0, 3, 1, 2, 4)

    q, k, v = to_chunks(q), to_chunks(k), to_chunks(v)
    g = g.reshape(bsz, n_chunks, CHUNK, heads).transpose(0, 3, 1, 2)
    beta = beta.reshape(bsz, n_chunks, CHUNK, heads).transpose(0, 3, 1, 2)
    G = jnp.cumsum(g, axis=-1)

    idx = jnp.arange(CHUNK)
    lower_incl = idx[:, None] >= idx[None, :]
    strict = idx[:, None] > idx[None, :]
    diff = G[..., :, None] - G[..., None, :]
    decay = jnp.where(lower_incl, jnp.exp(jnp.where(lower_incl, diff, 0.0)), 0.0)

    kb = k * beta[..., None]
    vb = v * beta[..., None]
    m = jnp.where(strict, jnp.einsum('bhncd,bhnsd->bhncs', kb, k) * decay, 0.0)
    eye = jnp.eye(CHUNK, dtype=jnp.float32)
    a_mat = eye + m
    t_mat = lax.linalg.triangular_solve(a_mat, jnp.broadcast_to(eye, a_mat.shape),
                                        left_side=True, lower=True)
    u = jnp.einsum('bhncs,bhnse->bhnce', t_mat, vb)
    w = jnp.einsum('bhncs,bhnsd->bhncd', t_mat, kb * jnp.exp(G)[..., None])
    qg = q * jnp.exp(G)[..., None]
    attn = jnp.einsum('bhncd,bhnsd->bhncs', q, k) * decay
    g_last = G[..., -1]
    k_dec = k * jnp.exp(g_last[..., None] - G)[..., None]

    xs = tuple(jnp.moveaxis(t, 2, 0) for t in (u, w, qg, attn, k_dec, g_last))

    def step(state, inp):
        u_n, w_n, qg_n, attn_n, kdec_n, gl_n = inp
        v_new = u_n - jnp.einsum('bhcd,bhde->bhce', w_n, state)
        o_n = (jnp.einsum('bhcd,bhde->bhce', qg_n, state)
               + jnp.einsum('bhcs,bhse->bhce', attn_n, v_new))
        state = (state * jnp.exp(gl_n)[..., None, None]
                 + jnp.einsum('bhcd,bhce->bhde', kdec_n, v_new))
        return state, o_n

    s0 = jnp.zeros((bsz, heads, dh, v.shape[-1]), jnp.float32)
    _, o = lax.scan(step, s0, xs)
    return o.transpose(1, 0, 3, 2, 4).reshape(bsz, seq, heads, -1)


def gated_deltanet(q_in, k_in, v_in, z, a, b, conv_w, a_log, dt_bias, out_norm):
    bsz, seq, _ = q_in.shape
    qkv = causal_conv_silu(jnp.concatenate([q_in, k_in, v_in], axis=-1), conv_w)
    q, k, v = jnp.split(qkv.astype(jnp.float32), 3, axis=-1)
    q = l2_norm(q.reshape(bsz, seq, GDN_HEADS, HEAD_DIM)) * (HEAD_DIM ** -0.5)
    k = l2_norm(k.reshape(bsz, seq, GDN_HEADS, HEAD_DIM))
    v = v.reshape(bsz, seq, GDN_HEADS, HEAD_DIM)
    g = -jnp.exp(a_log.astype(jnp.float32)) * jax.nn.softplus(
        a.astype(jnp.float32) + dt_bias.astype(jnp.float32))
    beta = jax.nn.sigmoid(b.astype(jnp.float32))
    o = chunked_gated_delta_rule(q, k, v, g, beta)
    zf = z.astype(jnp.float32).reshape(bsz, seq, GDN_HEADS, HEAD_DIM)
    o = rms_norm(o, out_norm) * jax.nn.silu(zf)
    return o.reshape(bsz, seq, GDN_WIDTH).astype(q_in.dtype)


def sliding_window_attention(q, k, v, q_norm, k_norm, sinks, positions):
    bsz, seq, _ = q.shape
    n_blk = seq // BLOCK
    q = rope(rms_norm(q.reshape(bsz, seq, SWA_Q_HEADS, HEAD_DIM), q_norm), positions)
    k = rope(rms_norm(k.reshape(bsz, seq, SWA_KV_HEADS, HEAD_DIM), k_norm), positions)
    v = v.reshape(bsz, seq, SWA_KV_HEADS, HEAD_DIM)

    qb = q.reshape(bsz, n_blk, BLOCK, SWA_KV_HEADS, SWA_GROUP, HEAD_DIM)

    def band(t):
        tb = t.reshape(bsz, n_blk, BLOCK, SWA_KV_HEADS, HEAD_DIM)
        prev = jnp.pad(tb[:, :-1], ((0, 0), (1, 0), (0, 0), (0, 0), (0, 0)))
        return jnp.concatenate([prev, tb], axis=2)

    k_band, v_band = band(k), band(v)
    scores = jnp.einsum('bnqhgd,bnkhd->bnhgqk', qb, k_band).astype(jnp.float32) * (HEAD_DIM ** -0.5)

    qi = jnp.arange(BLOCK)[:, None]
    kj = jnp.arange(2 * BLOCK)[None, :]
    rel = qi + BLOCK - kj
    in_window = (rel >= 0) & (rel < WINDOW)
    has_prev = jnp.arange(n_blk)[:, None, None] > 0
    valid = in_window[None] & (has_prev | (kj >= BLOCK)[None])
    scores = jnp.where(valid[None, :, None, None], scores, -jnp.inf)

    sink = sinks.astype(jnp.float32).reshape(SWA_KV_HEADS, SWA_GROUP)[None, None, :, :, None]
    mx = jnp.maximum(scores.max(axis=-1), sink)
    p = jnp.exp(scores - mx[..., None])
    denom = p.sum(axis=-1) + jnp.exp(sink - mx)
    p = p / denom[..., None]
    out = jnp.einsum('bnhgqk,bnkhd->bnqhgd', p, v_band.astype(jnp.float32))
    return out.reshape(bsz, seq, SWA_Q_WIDTH).astype(q.dtype)


def hierarchical_moe(h, w_group, b_group, w_router, b_router, w_gate, w_up, w_down):
    bsz, seq, d = h.shape
    t = h.reshape(-1, d)
    group_prob = jax.nn.softmax((t @ w_group + b_group).astype(jnp.float32), axis=-1)
    p_group, g_idx = lax.top_k(group_prob, 1)
    expert_logits = (t @ w_router + b_router).astype(jnp.float32).reshape(
        -1, N_GROUPS, EXPERTS_PER_GROUP)
    g_onehot = jax.nn.one_hot(g_idx[:, 0], N_GROUPS, dtype=jnp.float32)
    in_group = jnp.einsum('tg,tge->te', g_onehot, expert_logits)
    p_exp = jax.nn.softmax(in_group, axis=-1)
    w_top, e_idx = lax.top_k(p_exp, TOP_K)
    w_top = w_top / jnp.sum(w_top, axis=-1, keepdims=True) * p_group
    expert_ids = g_idx * EXPERTS_PER_GROUP + e_idx
    combine = jnp.sum(jax.nn.one_hot(expert_ids, N_EXPERTS, dtype=jnp.float32)
                      * w_top[..., None], axis=1)
    out = jnp.zeros(t.shape, jnp.float32)
    for e in range(N_EXPERTS):
        hid = jax.nn.silu(t @ w_gate[e]) * (t @ w_up[e])
        out = out + combine[:, e:e + 1] * (hid @ w_down[e]).astype(jnp.float32)
    return out.astype(h.dtype).reshape(bsz, seq, d)


def setup_inputs(seed: int = 0) -> dict:
    key = jax.random.key(seed)
    ks = jax.random.split(key, 24)
    f32 = jnp.float32
    nrm = lambda k, shape: jax.random.normal(k, shape, f32)
    dt = jnp.exp(jax.random.uniform(ks[6], (DEPTH, GDN_HEADS), f32,
                                    jnp.log(jnp.float32(1e-3)), jnp.log(jnp.float32(1e-1))))
    return {
        "x": nrm(ks[0], (BATCH, SEQ, D_MODEL)),
        "c": nrm(ks[1], (BATCH, D_MODEL)),
        "positions": jnp.broadcast_to(jnp.arange(SEQ, dtype=jnp.int32), (BATCH, SEQ)),
        "w_ada": nrm(ks[2], (DEPTH, D_MODEL, 6 * D_MODEL)) * (0.5 * D_MODEL ** -0.5),
        "b_ada": 0.01 * nrm(ks[3], (DEPTH, 6 * D_MODEL)),
        "norm_mix": 1.0 + 0.1 * nrm(ks[4], (DEPTH, D_MODEL)),
        "w_in": nrm(ks[5], (DEPTH, D_MODEL, IN_WIDTH)) * D_MODEL ** -0.5,
        "conv_w": nrm(ks[7], (DEPTH, CONV_K, 3 * GDN_WIDTH)) * CONV_K ** -0.5,
        "a_log": jnp.log(jax.random.uniform(ks[8], (DEPTH, GDN_HEADS), f32, 1.0, 16.0)),
        "dt_bias": jnp.log(jnp.expm1(dt)),
        "gdn_out_norm": 1.0 + 0.1 * nrm(ks[9], (DEPTH, HEAD_DIM)),
        "q_norm": 1.0 + 0.1 * nrm(ks[10], (DEPTH, HEAD_DIM)),
        "k_norm": 1.0 + 0.1 * nrm(ks[11], (DEPTH, HEAD_DIM)),
        "sinks": nrm(ks[12], (DEPTH, SWA_Q_HEADS)),
        "w_out": nrm(ks[13], (DEPTH, MIX_OUT_WIDTH, D_MODEL)) * MIX_OUT_WIDTH ** -0.5,
        "norm_ffn": 1.0 + 0.1 * nrm(ks[14], (DEPTH, D_MODEL)),
        "w_group": nrm(ks[15], (DEPTH, D_MODEL, N_GROUPS)) * D_MODEL ** -0.5,
        "b_group": 0.01 * nrm(ks[16], (DEPTH, N_GROUPS)),
        "w_router": nrm(ks[17], (DEPTH, D_MODEL, N_EXPERTS)) * D_MODEL ** -0.5,
        "b_router": 0.01 * nrm(ks[18], (DEPTH, N_EXPERTS)),
        "w_gate": nrm(ks[19], (DEPTH, N_EXPERTS, D_MODEL, D_EXPERT)) * D_MODEL ** -0.5,
        "w_up": nrm(ks[20], (DEPTH, N_EXPERTS, D_MODEL, D_EXPERT)) * D_MODEL ** -0.5,
        "w_down": nrm(ks[21], (DEPTH, N_EXPERTS, D_EXPERT, D_MODEL)) * D_EXPERT ** -0.5,
    }


def reference(x, c, positions, w_ada, b_ada, norm_mix, w_in, conv_w, a_log, dt_bias,
              gdn_out_norm, q_norm, k_norm, sinks, w_out, norm_ffn, w_group, b_group,
              w_router, b_router, w_gate, w_up, w_down):
    split_points = tuple(int(s) for s in np.cumsum(IN_WIDTHS)[:-1])
    c_act = jax.nn.silu(c)
    for l in range(DEPTH):
        mod = c_act @ w_ada[l] + b_ada[l]
        shift1, scale1, gate1, shift2, scale2, gate2 = jnp.split(mod, 6, axis=-1)

        h = modulate(rms_norm(x, norm_mix[l]), shift1, scale1)
        proj = h @ w_in[l]
        (g_q, g_k, g_v, g_z, g_a, g_b, s_q, s_k, s_v) = jnp.split(proj, split_points, axis=-1)
        gdn_out = gated_deltanet(g_q, g_k, g_v, g_z, g_a, g_b, conv_w[l], a_log[l],
                                 dt_bias[l], gdn_out_norm[l])
        swa_out = sliding_window_attention(s_q, s_k, s_v, q_norm[l], k_norm[l],
                                           sinks[l], positions)
        mixed = jnp.concatenate([gdn_out, swa_out], axis=-1) @ w_out[l]
        x = x + gate1[:, None, :] * mixed

        h2 = modulate(rms_norm(x, norm_ffn[l]), shift2, scale2)
        ffn = hierarchical_moe(h2, w_group[l], b_group[l], w_router[l], b_router[l],
                               w_gate[l], w_up[l], w_down[l])
        x = x + gate2[:, None, :] * ffn
    return x
```

```python
import functools

import numpy as np
import jax
import jax.numpy as jnp
from jax import lax
from jax.experimental import pallas as pl
from jax.experimental.pallas import tpu as pltpu

F32 = jnp.float32
BF16 = jnp.bfloat16

HEAD_DIM = 64
GDN_HEADS = 8
GDN_WIDTH = GDN_HEADS * HEAD_DIM
CONV_K = 4
CHUNK = 64
SWA_Q_HEADS = 8
SWA_KV_HEADS = 2
SWA_GROUP = SWA_Q_HEADS // SWA_KV_HEADS
SWA_WIDTH = SWA_Q_HEADS * HEAD_DIM
WINDOW = 128
BLOCK = 128
ROPE_THETA = 10000.0
N_GROUPS = 4
EXPERTS_PER_GROUP = 8
N_EXPERTS = N_GROUPS * EXPERTS_PER_GROUP
EPS = 1e-6
LANES = 128
HEADS_PER_TILE = 4
TILE = HEADS_PER_TILE * HEAD_DIM
VMEM_LIMIT = 56 * 1024 * 1024


def _dot(a, b):
    return jnp.dot(a, b, preferred_element_type=F32)


def _dot_nt(a, b):
    return lax.dot_general(a, b, (((1,), (1,)), ((), ())), preferred_element_type=F32)


def _dot_tn(a, b):
    return lax.dot_general(a, b, (((0,), (0,)), ((), ())), preferred_element_type=F32)


def _split(x, n):
    parts = []
    r = x
    for i in range(n):
        p = r.astype(BF16)
        parts.append(p)
        if i + 1 < n:
            r = r - p.astype(F32)
    return parts


def _dot_split_lhs(x, w, n):
    acc = None
    for p in _split(x, n):
        t = _dot(p, w)
        acc = t if acc is None else acc + t
    return acc


def _sigmoid(x):
    return 1.0 / (1.0 + jnp.exp(-x))


def _silu(x):
    return x * _sigmoid(x)


def _softplus(x):
    return jnp.maximum(x, 0.0) + jnp.log1p(jnp.exp(-jnp.abs(x)))


def _ada_kernel(c_ref, w_ref, b_ref, o_ref):
    act = _silu(c_ref[...])
    o_ref[...] = _dot(act.astype(BF16), w_ref[...].astype(BF16)) + b_ref[...]


def _ada(c, w_ada, b_ada):
    bsz, d = c.shape
    n = w_ada.shape[1]
    rows = 8
    cp = jnp.zeros((rows, d), F32).at[:bsz].set(c)
    tn = 1536
    out = pl.pallas_call(
        _ada_kernel,
        grid=(n // tn,),
        in_specs=[pl.BlockSpec((rows, d), lambda j: (0, 0)),
                  pl.BlockSpec((d, tn), lambda j: (0, j)),
                  pl.BlockSpec((1, tn), lambda j: (0, j))],
        out_specs=pl.BlockSpec((rows, tn), lambda j: (0, j)),
        out_shape=jax.ShapeDtypeStruct((rows, n), F32),
        compiler_params=pltpu.CompilerParams(vmem_limit_bytes=VMEM_LIMIT),
        name="ada",
    )(cp, w_ada, b_ada.reshape(1, n))
    return out[:bsz]


def _inproj_kernel(x_ref, sh_ref, sc_ref, nw_ref, pos_ref, invf_ref, qn_ref, kn_ref, bd512_ref,
                   wqkv_ref, wz_ref, wab_ref, wsq_ref, wsk_ref, wsv_ref,
                   qkv_ref, z_ref, ab_ref, sq_ref, sk_ref, sv_ref):
    x = x_ref[...]
    ms = jnp.mean(x * x, axis=-1, keepdims=True)
    h = x * lax.rsqrt(ms + EPS) * nw_ref[...]
    h = h * (1.0 + sc_ref[...]) + sh_ref[...]
    hb = h.astype(BF16)
    qkv_ref[...] = _dot(hb, wqkv_ref[...]).astype(BF16)
    z_ref[...] = _dot(hb, wz_ref[...]).astype(BF16)
    ab_ref[...] = _dot(hb, wab_ref[...])
    sv_ref[...] = _dot(hb, wsv_ref[...]).astype(BF16)

    bd512 = bd512_ref[...]
    ang = pos_ref[...].astype(F32) * invf_ref[...]
    reps = SWA_WIDTH // LANES
    cos = jnp.concatenate([jnp.cos(ang)] * reps, axis=1)
    sin = jnp.concatenate([jnp.sin(ang)] * reps, axis=1)
    lane = lax.broadcasted_iota(jnp.int32, cos.shape, 1)
    half = HEAD_DIM // 2
    first = (lane % HEAD_DIM) < half
    sin_signed = jnp.where(first, -sin, sin)

    def norm_rope(t, gain):
        ms = _dot_split_lhs(t * t, bd512, 2) * (1.0 / HEAD_DIM)
        t = t * lax.rsqrt(ms + EPS) * gain
        swapped = jnp.where(first, pltpu.roll(t, SWA_WIDTH - half, 1), pltpu.roll(t, half, 1))
        return t * cos + swapped * sin_signed

    sq_ref[...] = (norm_rope(_dot(hb, wsq_ref[...]), qn_ref[...]) * (HEAD_DIM ** -0.5)).astype(BF16)
    sk_ref[...] = norm_rope(_dot(hb, wsk_ref[...]), kn_ref[...]).astype(BF16)


def _inproj(x, shift, scale, norm_w, positions, q_norm, k_norm, wqkv, wz, wab, wsq, wsk, wsv, tm):
    bsz, seq, d = x.shape
    half = HEAD_DIM // 2
    inv_freq = jnp.power(jnp.float32(ROPE_THETA), -jnp.arange(half, dtype=F32) / half)
    invf = jnp.tile(inv_freq, LANES // half).reshape(1, LANES)
    qn = jnp.tile(q_norm, SWA_Q_HEADS).reshape(1, SWA_WIDTH)
    kn = jnp.tile(k_norm, SWA_Q_HEADS).reshape(1, SWA_WIDTH)
    h512 = np.arange(SWA_WIDTH) // HEAD_DIM
    bd512 = jnp.asarray((h512[:, None] == h512[None, :]).astype(np.float32), BF16)
    row = lambda b, i: (b, i, 0)
    vec = lambda b, i: (b, 0, 0)
    const = lambda b, i: (0, 0)
    widths = (3 * GDN_WIDTH, GDN_WIDTH, LANES, SWA_WIDTH, SWA_WIDTH, SWA_WIDTH)
    dtypes = (BF16, BF16, F32, BF16, BF16, BF16)
    return pl.pallas_call(
        _inproj_kernel,
        grid=(bsz, seq // tm),
        in_specs=[pl.BlockSpec((None, tm, d), row),
                  pl.BlockSpec((None, 1, d), vec),
                  pl.BlockSpec((None, 1, d), vec),
                  pl.BlockSpec((1, d), const),
                  pl.BlockSpec((None, tm, 1), row)]
                 + [pl.BlockSpec(w.shape, const)
                    for w in (invf, qn, kn, bd512, wqkv, wz, wab, wsq, wsk, wsv)],
        out_specs=[pl.BlockSpec((None, tm, w), row) for w in widths],
        out_shape=[jax.ShapeDtypeStruct((bsz, seq, w), t) for w, t in zip(widths, dtypes)],
        compiler_params=pltpu.CompilerParams(
            dimension_semantics=("parallel", "parallel"), vmem_limit_bytes=VMEM_LIMIT),
        name="inproj",
    )(x, shift, scale, norm_w, positions.reshape(bsz, seq, 1), invf, qn, kn, bd512,
      wqkv, wz, wab, wsq, wsk, wsv)


def _gdn_consts(tb):
    r = np.arange(CHUNK)[:, None]
    cidx = np.arange(TILE)[None, :] % HEAD_DIM
    low = (r >= cidx).astype(np.float32)
    strict = (r > cidx).astype(np.float32)
    eye = (r == cidx).astype(np.float32)
    upper = (r <= cidx).astype(np.float32)
    hb = np.arange(TILE) // HEAD_DIM
    blk = (hb[:, None] == hb[None, :]).astype(np.float32)
    h512 = np.arange(GDN_WIDTH) // HEAD_DIM
    bd512 = (h512[:, None] == h512[None, :]).astype(np.float32)
    e128 = np.zeros((LANES, 2 * GDN_WIDTH), np.float32)
    for j in range(2 * GDN_HEADS):
        e128[j, j * HEAD_DIM:(j + 1) * HEAD_DIM] = 1.0
    t = np.arange(tb)
    lbd = ((t[:, None] >= t[None, :]) & (t[:, None] // CHUNK == t[None, :] // CHUNK))
    return dict(low=jnp.asarray(low), strict=jnp.asarray(strict), eye=jnp.asarray(eye),
                upper=jnp.asarray(upper), blk=jnp.asarray(blk, BF16),
                bd512=jnp.asarray(bd512, BF16), e128=jnp.asarray(e128, BF16),
                lbd=jnp.asarray(lbd.astype(np.float32), BF16))


def _gdn_kernel(xin_ref, z_ref, ab_ref, cw_ref, alog_ref, dt_ref, gamma_ref,
                bd512_ref, e128_ref, lbd_ref, low_ref, strict_ref, eye_ref, upper_ref, blk_ref,
                out_ref,
                tail_ref, s_ref, q_s, k_s, v_s, g_s, b_s, gc_s, o_s, *, tb):
    step = pl.program_id(1)

    @pl.when(step == 0)
    def _():
        tail_ref[...] = jnp.zeros_like(tail_ref)
        s_ref[...] = jnp.zeros_like(s_ref)

    xf = xin_ref[...].astype(F32)
    xcat = jnp.concatenate([tail_ref[...], xf], axis=0)
    cw = cw_ref[...]
    y = xf * cw[CONV_K - 1:CONV_K]
    for j in range(CONV_K - 1):
        y = y + pltpu.roll(xcat, CONV_K - 1 - j, 0)[8:] * cw[j:j + 1]
    tail_ref[...] = xf[tb - 8:]
    y = _silu(y)

    bd512 = bd512_ref[...]
    q = y[:, :GDN_WIDTH]
    k = y[:, GDN_WIDTH:2 * GDN_WIDTH]
    q = q * lax.rsqrt(_dot_split_lhs(q * q, bd512, 2) + EPS) * (HEAD_DIM ** -0.5)
    k = k * lax.rsqrt(_dot_split_lhs(k * k, bd512, 2) + EPS)
    q_s[...] = q
    k_s[...] = k
    v_s[...] = y[:, 2 * GDN_WIDTH:]

    ab = ab_ref[...]
    lane = lax.broadcasted_iota(jnp.int32, ab.shape, 1)
    gb = jnp.where(lane < GDN_HEADS,
                   -jnp.exp(alog_ref[...]) * _softplus(ab + dt_ref[...]),
                   _sigmoid(ab))
    lbd = lbd_ref[...]
    gcum = None
    for p in _split(gb, 3):
        t = _dot(lbd, p)
        gcum = t if gcum is None else gcum + t
    e128 = e128_ref[...]
    gbe = _dot_split_lhs(gb, e128, 3)
    g_s[...] = gbe[:, :GDN_WIDTH]
    b_s[...] = gbe[:, GDN_WIDTH:]
    gc_s[...] = _dot_split_lhs(gcum, e128[:, :GDN_WIDTH], 3)

    low = low_ref[...] > 0.5
    strict = strict_ref[...] > 0.5
    upper = upper_ref[...] > 0.5
    eye = eye_ref[...]
    blk = blk_ref[...]
    blk_f = blk.astype(F32)

    def bdiag(t):
        tb16 = t.astype(BF16)
        return jnp.concatenate([tb16] * HEADS_PER_TILE, axis=0) * blk

    def chunk_body(c, carry):
        r0 = pl.multiple_of(c * CHUNK, CHUNK)
        rows = pl.ds(r0, CHUNK)
        for gp in range(GDN_WIDTH // TILE):
            sl = slice(gp * TILE, (gp + 1) * TILE)
            qc = q_s[rows, sl]
            kc = k_s[rows, sl]
            vc = v_s[rows, sl]
            bc = b_s[rows, sl]
            gcc = gc_s[rows, sl]
            grow = jnp.sum(jnp.where(upper, g_s[rows, sl], 0.0), axis=0, keepdims=True)
            decay = jnp.where(low, jnp.exp(jnp.where(low, gcc - grow, 0.0)), 0.0)
            kb = kc * bc
            vb = vc * bc
            eg = jnp.exp(gcc)
            glast = gcc[CHUNK - 1:CHUNK, :]
            kdec = kc * jnp.exp(glast - gcc)

            mm = _dot_nt(jnp.concatenate([kb, qc], axis=0).astype(BF16), bdiag(kc))
            m = jnp.where(strict, mm[:CHUNK] * decay, 0.0)
            attn = mm[CHUNK:] * decay

            pw = _dot(m.astype(BF16), bdiag(m))
            tinv = eye - m
            for _ in range(4):
                both = _dot(jnp.concatenate([pw, tinv], axis=0).astype(BF16), bdiag(pw))
                pw = both[:CHUNK]
                tinv = tinv + both[CHUNK:]
            tinv = tinv + _dot(tinv.astype(BF16), bdiag(pw))
            tinv16 = tinv.astype(BF16)

            u = _dot(tinv16, bdiag(vb))
            w = _dot(tinv16, bdiag(kb * eg))
            state = s_ref[gp]
            ws_qs = _dot(jnp.concatenate([w, qc * eg], axis=0).astype(BF16), state.astype(BF16))
            vnew = u - ws_qs[:CHUNK]
            o_s[rows, sl] = ws_qs[CHUNK:] + _dot(attn.astype(BF16), bdiag(vnew))
            kv = _dot_tn(kdec.astype(BF16), vnew.astype(BF16))
            s_ref[gp] = state * jnp.exp(glast) + kv * blk_f
        return carry

    lax.fori_loop(0, tb // CHUNK, chunk_body, 0)

    o = o_s[...]
    ms = _dot_split_lhs(o * o, bd512, 2) * (1.0 / HEAD_DIM)
    zf = z_ref[...].astype(F32)
    out_ref[...] = (o * lax.rsqrt(ms + EPS) * gamma_ref[...] * _silu(zf)).astype(BF16)


def _gdn(qkv, z, ab, conv_w, a_log, dt_bias, out_norm, tb):
    bsz, seq, _ = qkv.shape
    cs = _gdn_consts(tb)
    pad = lambda v: jnp.zeros((1, LANES), F32).at[0, :GDN_HEADS].set(v)
    gamma = jnp.tile(out_norm, GDN_HEADS).reshape(1, GDN_WIDTH)
    row = lambda b, i: (b, i, 0)
    const = lambda b, i: (0, 0)
    consts = (cs["bd512"], cs["e128"], cs["lbd"], cs["low"], cs["strict"], cs["eye"],
              cs["upper"], cs["blk"])
    small = (conv_w, pad(a_log), pad(dt_bias), gamma)
    wide = pltpu.VMEM((tb, GDN_WIDTH), F32)
    return pl.pallas_call(
        functools.partial(_gdn_kernel, tb=tb),
        grid=(bsz, seq // tb),
        in_specs=[pl.BlockSpec((None, tb, 3 * GDN_WIDTH), row),
                  pl.BlockSpec((None, tb, GDN_WIDTH), row),
                  pl.BlockSpec((None, tb, LANES), row)]
                 + [pl.BlockSpec(a.shape, const) for a in small + consts],
        out_specs=pl.BlockSpec((None, tb, GDN_WIDTH), row),
        out_shape=jax.ShapeDtypeStruct((bsz, seq, GDN_WIDTH), BF16),
        scratch_shapes=[pltpu.VMEM((8, 3 * GDN_WIDTH), F32),
                        pltpu.VMEM((GDN_WIDTH // TILE, TILE, TILE), F32),
                        wide, wide, wide, wide, wide, wide, wide],
        compiler_params=pltpu.CompilerParams(
            dimension_semantics=("parallel", "arbitrary"), vmem_limit_bytes=VMEM_LIMIT),
        name="gdn",
    )(qkv, z, ab, *small, *consts)


def _swa_kernel(q_ref, kp_ref, kc_ref, vp_ref, vc_ref, sink_ref, valid_ref, hmask_ref, out_ref):
    step = pl.program_id(1)
    q = q_ref[...].astype(F32)
    valid = (valid_ref[...] > 0.5) & ((step > 0) | (valid_ref[...] > 1.5))
    hmask = hmask_ref[...]
    for g in range(SWA_KV_HEADS):
        sl = slice(g * TILE, (g + 1) * TILE)
        qg = q[:, sl]
        kband = jnp.concatenate([kp_ref[:, sl], kc_ref[:, sl]], axis=0)
        vband = jnp.concatenate([vp_ref[:, sl], vc_ref[:, sl]], axis=0)
        lhs = jnp.concatenate([qg * hmask[h:h + 1] for h in range(SWA_GROUP)], axis=0)
        s = _dot_nt(lhs.astype(BF16), kband)
        s = jnp.where(valid, s, -jnp.inf)
        sink = sink_ref[g]
        mx = jnp.maximum(jnp.max(s, axis=1, keepdims=True), sink)
        p = jnp.exp(s - mx)
        denom = jnp.sum(p, axis=1, keepdims=True) + jnp.exp(sink - mx)
        p = (p / denom).astype(BF16)
        pcat = jnp.concatenate([p[h * BLOCK:(h + 1) * BLOCK] for h in range(SWA_GROUP)], axis=1)
        vband32 = vband.astype(F32)
        vcat = jnp.concatenate([(vband32 * hmask[h:h + 1]).astype(BF16)
                                for h in range(SWA_GROUP)], axis=0)
        out_ref[:, sl] = _dot(pcat, vcat).astype(BF16)


def _swa(sq, sk, sv, sinks):
    bsz, seq, _ = sq.shape
    sink_rows = jnp.repeat(sinks.astype(F32), BLOCK).reshape(SWA_KV_HEADS, SWA_GROUP * BLOCK, 1)
    qi = np.arange(BLOCK)[:, None]
    kj = np.arange(2 * BLOCK)[None, :]
    rel = qi + BLOCK - kj
    inwin = ((rel >= 0) & (rel < WINDOW)).astype(np.float32)
    valid = inwin * np.where(kj >= BLOCK, 2.0, 1.0)
    valid = jnp.asarray(np.tile(valid, (SWA_GROUP, 1)), F32)
    hm = (np.arange(TILE)[None, :] // HEAD_DIM == np.arange(SWA_GROUP)[:, None]).astype(np.float32)
    row = lambda b, i: (b, i, 0)
    prev = lambda b, i: (b, jnp.maximum(i - 1, 0), 0)
    const2 = lambda b, i: (0, 0)
    const3 = lambda b, i: (0, 0, 0)
    blk = (None, BLOCK, SWA_WIDTH)
    return pl.pallas_call(
        _swa_kernel,
        grid=(bsz, seq // BLOCK),
        in_specs=[pl.BlockSpec(blk, row),
                  pl.BlockSpec(blk, prev), pl.BlockSpec(blk, row),
                  pl.BlockSpec(blk, prev), pl.BlockSpec(blk, row),
                  pl.BlockSpec(sink_rows.shape, const3),
                  pl.BlockSpec(valid.shape, const2),
                  pl.BlockSpec(hm.shape, const2)],
        out_specs=pl.BlockSpec(blk, row),
        out_shape=jax.ShapeDtypeStruct((bsz, seq, SWA_WIDTH), BF16),
        compiler_params=pltpu.CompilerParams(
            dimension_semantics=("parallel", "parallel"), vmem_limit_bytes=VMEM_LIMIT),
        name="swa",
    )(sq, sk, sk, sv, sv, sink_rows, valid, jnp.asarray(hm))


def _outproj_kernel(gdn_ref, swa_ref, x_ref, gate_ref, sh_ref, sc_ref, nw_ref, wo1_ref, wo2_ref,
                    wgrp_ref, bgrp_ref, wrt_ref, brt_ref, x1_ref, h2_ref, comb_ref):
    mixed = _dot(gdn_ref[...], wo1_ref[...]) + _dot(swa_ref[...], wo2_ref[...])
    x1 = x_ref[...] + gate_ref[...] * mixed
    x1_ref[...] = x1
    ms = jnp.mean(x1 * x1, axis=-1, keepdims=True)
    h2 = x1 * lax.rsqrt(ms + EPS) * nw_ref[...]
    h2 = h2 * (1.0 + sc_ref[...]) + sh_ref[...]
    h2b = h2.astype(BF16)
    h2_ref[...] = h2b

    lg = _dot(h2b, wgrp_ref[...]) + bgrp_ref[...]
    lane = lax.broadcasted_iota(jnp.int32, lg.shape, 1).astype(F32)
    none = jnp.float32(LANES)
    lg = jnp.where(lane < N_GROUPS, lg, -jnp.inf)
    gmax = jnp.max(lg, axis=1, keepdims=True)
    gidx = jnp.min(jnp.where(lg == gmax, lane, none), axis=1, keepdims=True)
    p_group = 1.0 / jnp.sum(jnp.exp(lg - gmax), axis=1, keepdims=True)
    le = _dot(h2b, wrt_ref[...]) + brt_ref[...]
    lo = gidx * EXPERTS_PER_GROUP
    in_group = (lane >= lo) & (lane < lo + EXPERTS_PER_GROUP)
    le = jnp.where(in_group, le, -jnp.inf)
    pe = jnp.exp(le - jnp.max(le, axis=1, keepdims=True))
    pe = pe / jnp.sum(pe, axis=1, keepdims=True)
    p1 = jnp.max(pe, axis=1, keepdims=True)
    i1 = jnp.min(jnp.where(in_group & (pe == p1), lane, none), axis=1, keepdims=True)
    rest = in_group & (lane != i1)
    pe2 = jnp.where(rest, pe, -1.0)
    p2 = jnp.max(pe2, axis=1, keepdims=True)
    i2 = jnp.min(jnp.where(rest & (pe2 == p2), lane, none), axis=1, keepdims=True)
    wsum = p1 + p2
    comb_ref[...] = (jnp.where(lane == i1, p1 / wsum * p_group, 0.0)
                     + jnp.where(lane == i2, p2 / wsum * p_group, 0.0))


def _outproj(gdn, swa, x, gate, shift, scale, norm_w, wo1, wo2, wgrp, bgrp, wrt, brt, tm):
    bsz, seq, d = x.shape
    row = lambda b, i: (b, i, 0)
    vec = lambda b, i: (b, 0, 0)
    const = lambda b, i: (0, 0)
    return pl.pallas_call(
        _outproj_kernel,
        grid=(bsz, seq // tm),
        in_specs=[pl.BlockSpec((None, tm, GDN_WIDTH), row),
                  pl.BlockSpec((None, tm, SWA_WIDTH), row),
                  pl.BlockSpec((None, tm, d), row),
                  pl.BlockSpec((None, 1, d), vec),
                  pl.BlockSpec((None, 1, d), vec),
                  pl.BlockSpec((None, 1, d), vec),
                  pl.BlockSpec((1, d), const)]
                 + [pl.BlockSpec(a.shape, const) for a in (wo1, wo2, wgrp, bgrp, wrt, brt)],
        out_specs=[pl.BlockSpec((None, tm, d), row),
                   pl.BlockSpec((None, tm, d), row),
                   pl.BlockSpec((None, tm, LANES), row)],
        out_shape=[jax.ShapeDtypeStruct((bsz, seq, d), F32),
                   jax.ShapeDtypeStruct((bsz, seq, d), BF16),
                   jax.ShapeDtypeStruct((bsz, seq, LANES), F32)],
        compiler_params=pltpu.CompilerParams(
            dimension_semantics=("parallel", "parallel"), vmem_limit_bytes=VMEM_LIMIT),
        name="outproj",
    )(gdn, swa, x, gate, shift, scale, norm_w, wo1, wo2, wgrp, bgrp, wrt, brt)


def _moe_kernel(h2_ref, comb_ref, x1_ref, gate_ref, wg_ref, wu_ref, wd_ref, out_ref, acc_ref):
    e = pl.program_id(2)

    @pl.when(e == 0)
    def _():
        acc_ref[...] = jnp.zeros_like(acc_ref)

    h2 = h2_ref[...]
    hid = _silu(_dot(h2, wg_ref[...])) * _dot(h2, wu_ref[...])
    y = _dot(hid.astype(BF16), wd_ref[...])
    comb = comb_ref[...]
    lane = lax.broadcasted_iota(jnp.int32, comb.shape, 1)
    col = jnp.sum(jnp.where(lane == e, comb, 0.0), axis=1, keepdims=True)
    acc_ref[...] += col * y

    @pl.when(e == pl.num_programs(2) - 1)
    def _():
        out_ref[...] = x1_ref[...] + gate_ref[...] * acc_ref[...]


def _moe(h2, comb, x1, gate, wg, wu, wd, tm):
    bsz, seq, d = x1.shape
    n_e, _, de = wg.shape
    row = lambda b, i, e: (b, i, 0)
    vec = lambda b, i, e: (b, 0, 0)
    wsel = lambda b, i, e: (e, 0, 0)
    return pl.pallas_call(
        _moe_kernel,
        grid=(bsz, seq // tm, n_e),
        in_specs=[pl.BlockSpec((None, tm, d), row),
                  pl.BlockSpec((None, tm, LANES), row),
                  pl.BlockSpec((None, tm, d), row),
                  pl.BlockSpec((None, 1, d), vec),
                  pl.BlockSpec((None, d, de), wsel),
                  pl.BlockSpec((None, d, de), wsel),
                  pl.BlockSpec((None, de, d), wsel)],
        out_specs=pl.BlockSpec((None, tm, d), row),
        out_shape=jax.ShapeDtypeStruct((bsz, seq, d), F32),
        scratch_shapes=[pltpu.VMEM((tm, d), F32)],
        compiler_params=pltpu.CompilerParams(
            dimension_semantics=("parallel", "parallel", "arbitrary"),
            vmem_limit_bytes=VMEM_LIMIT),
        name="moe",
    )(h2, comb, x1, gate, wg, wu, wd)


def _layer(x, c_mod, positions, norm_mix, w_in, conv_w, a_log, dt_bias, gdn_out_norm, q_norm,
           k_norm, sinks, w_out, norm_ffn, w_group, b_group, w_router, b_router, w_gate, w_up,
           w_down):
    bsz, seq, d = x.shape
    shift1, scale1, gate1, shift2, scale2, gate2 = [
        m.reshape(bsz, 1, d) for m in jnp.split(c_mod, 6, axis=-1)]

    o = 0
    w_qkv = w_in[:, o:o + 3 * GDN_WIDTH]; o += 3 * GDN_WIDTH
    w_z = w_in[:, o:o + GDN_WIDTH]; o += GDN_WIDTH
    w_ab = w_in[:, o:o + 2 * GDN_HEADS]; o += 2 * GDN_HEADS
    w_sq = w_in[:, o:o + SWA_WIDTH]; o += SWA_WIDTH
    kvw = SWA_KV_HEADS * HEAD_DIM
    w_sk = w_in[:, o:o + kvw]; o += kvw
    w_sv = w_in[:, o:o + kvw]
    rep = lambda w: jnp.repeat(w.reshape(d, SWA_KV_HEADS, 1, HEAD_DIM), SWA_GROUP, axis=2
                               ).reshape(d, SWA_WIDTH)
    w_ab = jnp.zeros((d, LANES), F32).at[:, :2 * GDN_HEADS].set(w_ab)
    tm = min(512, seq)
    qkv, z, ab, sq, sk, sv = _inproj(
        x, shift1, scale1, norm_mix.reshape(1, d), positions, q_norm, k_norm,
        w_qkv.astype(BF16), w_z.astype(BF16),
        w_ab.astype(BF16), w_sq.astype(BF16), rep(w_sk).astype(BF16), rep(w_sv).astype(BF16), tm)

    gdn = _gdn(qkv, z, ab, conv_w, a_log, dt_bias, gdn_out_norm, min(256, seq))
    swa = _swa(sq, sk, sv, sinks)

    padw = lambda w: jnp.zeros((d, LANES), F32).at[:, :w.shape[1]].set(w).astype(BF16)
    padb = lambda b: jnp.zeros((1, LANES), F32).at[0, :b.shape[0]].set(b)
    x1, h2, comb = _outproj(
        gdn, swa, x, gate1, shift2, scale2, norm_ffn.reshape(1, d),
        w_out[:GDN_WIDTH].astype(BF16), w_out[GDN_WIDTH:].astype(BF16),
        padw(w_group), padb(b_group), padw(w_router), padb(b_router), tm)

    return _moe(h2, comb, x1, gate2, w_gate.astype(BF16), w_up.astype(BF16),
                w_down.astype(BF16), min(1024, seq))


def kernel(x, c, positions, w_ada, b_ada, norm_mix, w_in, conv_w, a_log, dt_bias, gdn_out_norm,
           q_norm, k_norm, sinks, w_out, norm_ffn, w_group, b_group, w_router, b_router, w_gate,
           w_up, w_down):
    depth = w_ada.shape[0]
    for l in range(depth):
        c_mod = _ada(c, w_ada[l], b_ada[l])
        x = _layer(x, c_mod, positions, norm_mix[l], w_in[l], conv_w[l], a_log[l], dt_bias[l],
                   gdn_out_norm[l], q_norm[l], k_norm[l], sinks[l], w_out[l], norm_ffn[l],
                   w_group[l], b_group[l], w_router[l], b_router[l], w_gate[l], w_up[l],
                   w_down[l])
    return x
```

```python
import functools

import numpy as np
import jax
import jax.numpy as jnp
from jax import lax
from jax.experimental import pallas as pl
from jax.experimental.pallas import tpu as pltpu

F32 = jnp.float32
BF16 = jnp.bfloat16

HEAD_DIM = 64
GDN_HEADS = 8
GDN_WIDTH = GDN_HEADS * HEAD_DIM
CONV_K = 4
CHUNK = 64
SWA_Q_HEADS = 8
SWA_KV_HEADS = 2
SWA_GROUP = SWA_Q_HEADS // SWA_KV_HEADS
SWA_WIDTH = SWA_Q_HEADS * HEAD_DIM
WINDOW = 128
BLOCK = 128
ROPE_THETA = 10000.0
N_GROUPS = 4
EXPERTS_PER_GROUP = 8
N_EXPERTS = N_GROUPS * EXPERTS_PER_GROUP
EPS = 1e-6
LANES = 128
W1_LANE = 2 * N_EXPERTS
MOE_TILE = 256
HEADS_PER_TILE = 4
TILE = HEADS_PER_TILE * HEAD_DIM
VMEM_LIMIT = 56 * 1024 * 1024


def _dot(a, b):
    return jnp.dot(a, b, preferred_element_type=F32)


def _dot_nt(a, b):
    return lax.dot_general(a, b, (((1,), (1,)), ((), ())), preferred_element_type=F32)


def _dot_tn(a, b):
    return lax.dot_general(a, b, (((0,), (0,)), ((), ())), preferred_element_type=F32)


def _split(x, n):
    parts = []
    r = x
    for i in range(n):
        p = r.astype(BF16)
        parts.append(p)
        if i + 1 < n:
            r = r - p.astype(F32)
    return parts


def _dot_split_lhs(x, w, n):
    acc = None
    for p in _split(x, n):
        t = _dot(p, w)
        acc = t if acc is None else acc + t
    return acc


def _sigmoid(x):
    return 1.0 / (1.0 + jnp.exp(-x))


def _silu(x):
    return x * _sigmoid(x)


def _softplus(x):
    return jnp.maximum(x, 0.0) + jnp.log1p(jnp.exp(-jnp.abs(x)))


def _ada_kernel(c_ref, w_ref, b_ref, o_ref):
    act = _silu(c_ref[...])
    o_ref[...] = _dot(act.astype(BF16), w_ref[...].astype(BF16)) + b_ref[...]


def _ada(c, w_ada, b_ada):
    bsz, d = c.shape
    n = w_ada.shape[1]
    rows = 8
    cp = jnp.zeros((rows, d), F32).at[:bsz].set(c)
    tn = 1536
    out = pl.pallas_call(
        _ada_kernel,
        grid=(n // tn,),
        in_specs=[pl.BlockSpec((rows, d), lambda j: (0, 0)),
                  pl.BlockSpec((d, tn), lambda j: (0, j)),
                  pl.BlockSpec((1, tn), lambda j: (0, j))],
        out_specs=pl.BlockSpec((rows, tn), lambda j: (0, j)),
        out_shape=jax.ShapeDtypeStruct((rows, n), F32),
        compiler_params=pltpu.CompilerParams(vmem_limit_bytes=VMEM_LIMIT),
        name="ada",
    )(cp, w_ada, b_ada.reshape(1, n))
    return out[:bsz]


def _inproj_kernel(x_ref, sh_ref, sc_ref, nw_ref, pos_ref, invf_ref, qn_ref, kn_ref, bd512_ref,
                   wqkv_ref, wz_ref, wab_ref, wsq_ref, wsk_ref, wsv_ref,
                   qkv_ref, z_ref, ab_ref, sq_ref, sk_ref, sv_ref):
    x = x_ref[...]
    ms = jnp.mean(x * x, axis=-1, keepdims=True)
    h = x * lax.rsqrt(ms + EPS) * nw_ref[...]
    h = h * (1.0 + sc_ref[...]) + sh_ref[...]
    hb = h.astype(BF16)
    qkv_ref[...] = _dot(hb, wqkv_ref[...]).astype(BF16)
    z_ref[...] = _dot(hb, wz_ref[...]).astype(BF16)
    ab_ref[...] = _dot(hb, wab_ref[...])
    sv_ref[...] = _dot(hb, wsv_ref[...]).astype(BF16)

    bd512 = bd512_ref[...]
    ang = pos_ref[...].astype(F32) * invf_ref[...]
    reps = SWA_WIDTH // LANES
    cos = jnp.concatenate([jnp.cos(ang)] * reps, axis=1)
    sin = jnp.concatenate([jnp.sin(ang)] * reps, axis=1)
    lane = lax.broadcasted_iota(jnp.int32, cos.shape, 1)
    half = HEAD_DIM // 2
    first = (lane % HEAD_DIM) < half
    sin_signed = jnp.where(first, -sin, sin)

    def norm_rope(t, gain):
        ms = _dot_split_lhs(t * t, bd512, 2) * (1.0 / HEAD_DIM)
        t = t * lax.rsqrt(ms + EPS) * gain
        swapped = jnp.where(first, pltpu.roll(t, SWA_WIDTH - half, 1), pltpu.roll(t, half, 1))
        return t * cos + swapped * sin_signed

    sq_ref[...] = (norm_rope(_dot(hb, wsq_ref[...]), qn_ref[...]) * (HEAD_DIM ** -0.5)).astype(BF16)
    sk_ref[...] = norm_rope(_dot(hb, wsk_ref[...]), kn_ref[...]).astype(BF16)


def _inproj(x, shift, scale, norm_w, positions, q_norm, k_norm, wqkv, wz, wab, wsq, wsk, wsv, tm):
    bsz, seq, d = x.shape
    half = HEAD_DIM // 2
    inv_freq = jnp.power(jnp.float32(ROPE_THETA), -jnp.arange(half, dtype=F32) / half)
    invf = jnp.tile(inv_freq, LANES // half).reshape(1, LANES)
    qn = jnp.tile(q_norm, SWA_Q_HEADS).reshape(1, SWA_WIDTH)
    kn = jnp.tile(k_norm, SWA_Q_HEADS).reshape(1, SWA_WIDTH)
    h512 = np.arange(SWA_WIDTH) // HEAD_DIM
    bd512 = jnp.asarray((h512[:, None] == h512[None, :]).astype(np.float32), BF16)
    row = lambda b, i: (b, i, 0)
    vec = lambda b, i: (b, 0, 0)
    const = lambda b, i: (0, 0)
    widths = (3 * GDN_WIDTH, GDN_WIDTH, LANES, SWA_WIDTH, SWA_WIDTH, SWA_WIDTH)
    dtypes = (BF16, BF16, F32, BF16, BF16, BF16)
    return pl.pallas_call(
        _inproj_kernel,
        grid=(bsz, seq // tm),
        in_specs=[pl.BlockSpec((None, tm, d), row),
                  pl.BlockSpec((None, 1, d), vec),
                  pl.BlockSpec((None, 1, d), vec),
                  pl.BlockSpec((1, d), const),
                  pl.BlockSpec((None, tm, 1), row)]
                 + [pl.BlockSpec(w.shape, const)
                    for w in (invf, qn, kn, bd512, wqkv, wz, wab, wsq, wsk, wsv)],
        out_specs=[pl.BlockSpec((None, tm, w), row) for w in widths],
        out_shape=[jax.ShapeDtypeStruct((bsz, seq, w), t) for w, t in zip(widths, dtypes)],
        compiler_params=pltpu.CompilerParams(
            dimension_semantics=("parallel", "parallel"), vmem_limit_bytes=VMEM_LIMIT),
        name="inproj",
    )(x, shift, scale, norm_w, positions.reshape(bsz, seq, 1), invf, qn, kn, bd512,
      wqkv, wz, wab, wsq, wsk, wsv)


def _gdn_consts(tb):
    r = np.arange(CHUNK)[:, None]
    cidx = np.arange(TILE)[None, :] % HEAD_DIM
    low = (r >= cidx).astype(np.float32)
    strict = (r > cidx).astype(np.float32)
    eye = (r == cidx).astype(np.float32)
    upper = (r <= cidx).astype(np.float32)
    hb = np.arange(TILE) // HEAD_DIM
    blk = (hb[:, None] == hb[None, :]).astype(np.float32)
    h512 = np.arange(GDN_WIDTH) // HEAD_DIM
    bd512 = (h512[:, None] == h512[None, :]).astype(np.float32)
    e128 = np.zeros((LANES, 2 * GDN_WIDTH), np.float32)
    for j in range(2 * GDN_HEADS):
        e128[j, j * HEAD_DIM:(j + 1) * HEAD_DIM] = 1.0
    t = np.arange(tb)
    lbd = ((t[:, None] >= t[None, :]) & (t[:, None] // CHUNK == t[None, :] // CHUNK))
    return dict(low=jnp.asarray(low), strict=jnp.asarray(strict), eye=jnp.asarray(eye),
                upper=jnp.asarray(upper), blk=jnp.asarray(blk, BF16),
                bd512=jnp.asarray(bd512, BF16), e128=jnp.asarray(e128, BF16),
                lbd=jnp.asarray(lbd.astype(np.float32), BF16))


def _gdn_kernel(xin_ref, z_ref, ab_ref, cw_ref, alog_ref, dt_ref, gamma_ref,
                bd512_ref, e128_ref, lbd_ref, low_ref, strict_ref, eye_ref, upper_ref, blk_ref,
                out_ref,
                tail_ref, s_ref, q_s, k_s, v_s, g_s, b_s, gc_s, o_s, *, tb):
    step = pl.program_id(1)

    @pl.when(step == 0)
    def _():
        tail_ref[...] = jnp.zeros_like(tail_ref)
        s_ref[...] = jnp.zeros_like(s_ref)

    xf = xin_ref[...].astype(F32)
    xcat = jnp.concatenate([tail_ref[...], xf], axis=0)
    cw = cw_ref[...]
    y = xf * cw[CONV_K - 1:CONV_K]
    for j in range(CONV_K - 1):
        y = y + pltpu.roll(xcat, CONV_K - 1 - j, 0)[8:] * cw[j:j + 1]
    tail_ref[...] = xf[tb - 8:]
    y = _silu(y)

    bd512 = bd512_ref[...]
    q = y[:, :GDN_WIDTH]
    k = y[:, GDN_WIDTH:2 * GDN_WIDTH]
    q = q * lax.rsqrt(_dot_split_lhs(q * q, bd512, 2) + EPS) * (HEAD_DIM ** -0.5)
    k = k * lax.rsqrt(_dot_split_lhs(k * k, bd512, 2) + EPS)
    q_s[...] = q
    k_s[...] = k
    v_s[...] = y[:, 2 * GDN_WIDTH:]

    ab = ab_ref[...]
    lane = lax.broadcasted_iota(jnp.int32, ab.shape, 1)
    gb = jnp.where(lane < GDN_HEADS,
                   -jnp.exp(alog_ref[...]) * _softplus(ab + dt_ref[...]),
                   _sigmoid(ab))
    lbd = lbd_ref[...]
    gcum = None
    for p in _split(gb, 3):
        t = _dot(lbd, p)
        gcum = t if gcum is None else gcum + t
    e128 = e128_ref[...]
    gbe = _dot_split_lhs(gb, e128, 3)
    g_s[...] = gbe[:, :GDN_WIDTH]
    b_s[...] = gbe[:, GDN_WIDTH:]
    gc_s[...] = _dot_split_lhs(gcum, e128[:, :GDN_WIDTH], 3)

    low = low_ref[...] > 0.5
    strict = strict_ref[...] > 0.5
    upper = upper_ref[...] > 0.5
    eye = eye_ref[...]
    blk = blk_ref[...]
    blk_f = blk.astype(F32)

    def bdiag(t):
        tb16 = t.astype(BF16)
        return jnp.concatenate([tb16] * HEADS_PER_TILE, axis=0) * blk

    def chunk_body(c, carry):
        r0 = pl.multiple_of(c * CHUNK, CHUNK)
        rows = pl.ds(r0, CHUNK)
        for gp in range(GDN_WIDTH // TILE):
            sl = slice(gp * TILE, (gp + 1) * TILE)
            qc = q_s[rows, sl]
            kc = k_s[rows, sl]
            vc = v_s[rows, sl]
            bc = b_s[rows, sl]
            gcc = gc_s[rows, sl]
            grow = jnp.sum(jnp.where(upper, g_s[rows, sl], 0.0), axis=0, keepdims=True)
            decay = jnp.where(low, jnp.exp(jnp.where(low, gcc - grow, 0.0)), 0.0)
            kb = kc * bc
            vb = vc * bc
            eg = jnp.exp(gcc)
            glast = gcc[CHUNK - 1:CHUNK, :]
            kdec = kc * jnp.exp(glast - gcc)

            mm = _dot_nt(jnp.concatenate([kb, qc], axis=0).astype(BF16), bdiag(kc))
            m = jnp.where(strict, mm[:CHUNK] * decay, 0.0)
            attn = mm[CHUNK:] * decay

            pw = _dot(m.astype(BF16), bdiag(m))
            tinv = eye - m
            for _ in range(4):
                both = _dot(jnp.concatenate([pw, tinv], axis=0).astype(BF16), bdiag(pw))
                pw = both[:CHUNK]
                tinv = tinv + both[CHUNK:]
            tinv = tinv + _dot(tinv.astype(BF16), bdiag(pw))
            tinv16 = tinv.astype(BF16)

            u = _dot(tinv16, bdiag(vb))
            w = _dot(tinv16, bdiag(kb * eg))
            state = s_ref[gp]
            ws_qs = _dot(jnp.concatenate([w, qc * eg], axis=0).astype(BF16), state.astype(BF16))
            vnew = u - ws_qs[:CHUNK]
            o_s[rows, sl] = ws_qs[CHUNK:] + _dot(attn.astype(BF16), bdiag(vnew))
            kv = _dot_tn(kdec.astype(BF16), vnew.astype(BF16))
            s_ref[gp] = state * jnp.exp(glast) + kv * blk_f
        return carry

    lax.fori_loop(0, tb // CHUNK, chunk_body, 0)

    o = o_s[...]
    ms = _dot_split_lhs(o * o, bd512, 2) * (1.0 / HEAD_DIM)
    zf = z_ref[...].astype(F32)
    out_ref[...] = (o * lax.rsqrt(ms + EPS) * gamma_ref[...] * _silu(zf)).astype(BF16)


def _gdn(qkv, z, ab, conv_w, a_log, dt_bias, out_norm, tb):
    bsz, seq, _ = qkv.shape
    cs = _gdn_consts(tb)
    pad = lambda v: jnp.zeros((1, LANES), F32).at[0, :GDN_HEADS].set(v)
    gamma = jnp.tile(out_norm, GDN_HEADS).reshape(1, GDN_WIDTH)
    row = lambda b, i: (b, i, 0)
    const = lambda b, i: (0, 0)
    consts = (cs["bd512"], cs["e128"], cs["lbd"], cs["low"], cs["strict"], cs["eye"],
              cs["upper"], cs["blk"])
    small = (conv_w, pad(a_log), pad(dt_bias), gamma)
    wide = pltpu.VMEM((tb, GDN_WIDTH), F32)
    return pl.pallas_call(
        functools.partial(_gdn_kernel, tb=tb),
        grid=(bsz, seq // tb),
        in_specs=[pl.BlockSpec((None, tb, 3 * GDN_WIDTH), row),
                  pl.BlockSpec((None, tb, GDN_WIDTH), row),
                  pl.BlockSpec((None, tb, LANES), row)]
                 + [pl.BlockSpec(a.shape, const) for a in small + consts],
        out_specs=pl.BlockSpec((None, tb, GDN_WIDTH), row),
        out_shape=jax.ShapeDtypeStruct((bsz, seq, GDN_WIDTH), BF16),
        scratch_shapes=[pltpu.VMEM((8, 3 * GDN_WIDTH), F32),
                        pltpu.VMEM((GDN_WIDTH // TILE, TILE, TILE), F32),
                        wide, wide, wide, wide, wide, wide, wide],
        compiler_params=pltpu.CompilerParams(
            dimension_semantics=("parallel", "arbitrary"), vmem_limit_bytes=VMEM_LIMIT),
        name="gdn",
    )(qkv, z, ab, *small, *consts)


def _swa_kernel(q_ref, kp_ref, kc_ref, vp_ref, vc_ref, sink_ref, valid_ref, hmask_ref, out_ref):
    step = pl.program_id(1)
    q = q_ref[...].astype(F32)
    valid = (valid_ref[...] > 0.5) & ((step > 0) | (valid_ref[...] > 1.5))
    hmask = hmask_ref[...]
    for g in range(SWA_KV_HEADS):
        sl = slice(g * TILE, (g + 1) * TILE)
        qg = q[:, sl]
        kband = jnp.concatenate([kp_ref[:, sl], kc_ref[:, sl]], axis=0)
        vband = jnp.concatenate([vp_ref[:, sl], vc_ref[:, sl]], axis=0)
        lhs = jnp.concatenate([qg * hmask[h:h + 1] for h in range(SWA_GROUP)], axis=0)
        s = _dot_nt(lhs.astype(BF16), kband)
        s = jnp.where(valid, s, -jnp.inf)
        sink = sink_ref[g]
        mx = jnp.maximum(jnp.max(s, axis=1, keepdims=True), sink)
        p = jnp.exp(s - mx)
        denom = jnp.sum(p, axis=1, keepdims=True) + jnp.exp(sink - mx)
        p = (p / denom).astype(BF16)
        pcat = jnp.concatenate([p[h * BLOCK:(h + 1) * BLOCK] for h in range(SWA_GROUP)], axis=1)
        vband32 = vband.astype(F32)
        vcat = jnp.concatenate([(vband32 * hmask[h:h + 1]).astype(BF16)
                                for h in range(SWA_GROUP)], axis=0)
        out_ref[:, sl] = _dot(pcat, vcat).astype(BF16)


def _swa(sq, sk, sv, sinks):
    bsz, seq, _ = sq.shape
    sink_rows = jnp.repeat(sinks.astype(F32), BLOCK).reshape(SWA_KV_HEADS, SWA_GROUP * BLOCK, 1)
    qi = np.arange(BLOCK)[:, None]
    kj = np.arange(2 * BLOCK)[None, :]
    rel = qi + BLOCK - kj
    inwin = ((rel >= 0) & (rel < WINDOW)).astype(np.float32)
    valid = inwin * np.where(kj >= BLOCK, 2.0, 1.0)
    valid = jnp.asarray(np.tile(valid, (SWA_GROUP, 1)), F32)
    hm = (np.arange(TILE)[None, :] // HEAD_DIM == np.arange(SWA_GROUP)[:, None]).astype(np.float32)
    row = lambda b, i: (b, i, 0)
    prev = lambda b, i: (b, jnp.maximum(i - 1, 0), 0)
    const2 = lambda b, i: (0, 0)
    const3 = lambda b, i: (0, 0, 0)
    blk = (None, BLOCK, SWA_WIDTH)
    return pl.pallas_call(
        _swa_kernel,
        grid=(bsz, seq // BLOCK),
        in_specs=[pl.BlockSpec(blk, row),
                  pl.BlockSpec(blk, prev), pl.BlockSpec(blk, row),
                  pl.BlockSpec(blk, prev), pl.BlockSpec(blk, row),
                  pl.BlockSpec(sink_rows.shape, const3),
                  pl.BlockSpec(valid.shape, const2),
                  pl.BlockSpec(hm.shape, const2)],
        out_specs=pl.BlockSpec(blk, row),
        out_shape=jax.ShapeDtypeStruct((bsz, seq, SWA_WIDTH), BF16),
        compiler_params=pltpu.CompilerParams(
            dimension_semantics=("parallel", "parallel"), vmem_limit_bytes=VMEM_LIMIT),
        name="swa",
    )(sq, sk, sk, sv, sv, sink_rows, valid, jnp.asarray(hm))


def _outproj_kernel(gdn_ref, swa_ref, x_ref, gate_ref, sh_ref, sc_ref, nw_ref, wo1_ref, wo2_ref,
                    wgrp_ref, bgrp_ref, wrt_ref, brt_ref, x1_ref, h2_ref, comb_ref):
    mixed = _dot(gdn_ref[...], wo1_ref[...]) + _dot(swa_ref[...], wo2_ref[...])
    x1 = x_ref[...] + gate_ref[...] * mixed
    x1_ref[...] = x1
    ms = jnp.mean(x1 * x1, axis=-1, keepdims=True)
    h2 = x1 * lax.rsqrt(ms + EPS) * nw_ref[...]
    h2 = h2 * (1.0 + sc_ref[...]) + sh_ref[...]
    h2b = h2.astype(BF16)
    h2_ref[...] = h2

    lg = _dot(h2b, wgrp_ref[...]) + bgrp_ref[...]
    lane = lax.broadcasted_iota(jnp.int32, lg.shape, 1).astype(F32)
    none = jnp.float32(LANES)
    lg = jnp.where(lane < N_GROUPS, lg, -jnp.inf)
    gmax = jnp.max(lg, axis=1, keepdims=True)
    gidx = jnp.min(jnp.where(lg == gmax, lane, none), axis=1, keepdims=True)
    p_group = 1.0 / jnp.sum(jnp.exp(lg - gmax), axis=1, keepdims=True)
    le = _dot(h2b, wrt_ref[...]) + brt_ref[...]
    lo = gidx * EXPERTS_PER_GROUP
    in_group = (lane >= lo) & (lane < lo + EXPERTS_PER_GROUP)
    le = jnp.where(in_group, le, -jnp.inf)
    pe = jnp.exp(le - jnp.max(le, axis=1, keepdims=True))
    pe = pe / jnp.sum(pe, axis=1, keepdims=True)
    p1 = jnp.max(pe, axis=1, keepdims=True)
    i1 = jnp.min(jnp.where(in_group & (pe == p1), lane, none), axis=1, keepdims=True)
    rest = in_group & (lane != i1)
    pe2 = jnp.where(rest, pe, -1.0)
    p2 = jnp.max(pe2, axis=1, keepdims=True)
    i2 = jnp.min(jnp.where(rest & (pe2 == p2), lane, none), axis=1, keepdims=True)
    wsum = p1 + p2
    comb_ref[...] = (jnp.where(lane == i1, 1.0, 0.0)
                     + jnp.where(lane == i2 + N_EXPERTS, 1.0, 0.0)
                     + jnp.where(lane == W1_LANE, p1 / wsum * p_group, 0.0)
                     + jnp.where(lane == W1_LANE + 1, p2 / wsum * p_group, 0.0))


def _outproj(gdn, swa, x, gate, shift, scale, norm_w, wo1, wo2, wgrp, bgrp, wrt, brt, tm):
    bsz, seq, d = x.shape
    row = lambda b, i: (b, i, 0)
    vec = lambda b, i: (b, 0, 0)
    const = lambda b, i: (0, 0)
    return pl.pallas_call(
        _outproj_kernel,
        grid=(bsz, seq // tm),
        in_specs=[pl.BlockSpec((None, tm, GDN_WIDTH), row),
                  pl.BlockSpec((None, tm, SWA_WIDTH), row),
                  pl.BlockSpec((None, tm, d), row),
                  pl.BlockSpec((None, 1, d), vec),
                  pl.BlockSpec((None, 1, d), vec),
                  pl.BlockSpec((None, 1, d), vec),
                  pl.BlockSpec((1, d), const)]
                 + [pl.BlockSpec(a.shape, const) for a in (wo1, wo2, wgrp, bgrp, wrt, brt)],
        out_specs=[pl.BlockSpec((None, tm, d), row),
                   pl.BlockSpec((None, tm, d), row),
                   pl.BlockSpec((None, tm, LANES), row)],
        out_shape=[jax.ShapeDtypeStruct((bsz, seq, d), F32),
                   jax.ShapeDtypeStruct((bsz, seq, d), F32),
                   jax.ShapeDtypeStruct((bsz, seq, LANES), F32)],
        compiler_params=pltpu.CompilerParams(
            dimension_semantics=("parallel", "parallel"), vmem_limit_bytes=VMEM_LIMIT),
        name="outproj",
    )(gdn, swa, x, gate, shift, scale, norm_w, wo1, wo2, wgrp, bgrp, wrt, brt)


def _route_kernel(info_ref, lstrict_ref, ustrict_ref, sel_ref, pos_ref, cnt_ref,
                  total_ref, run_ref, off_ref, *, tile):
    phase = pl.program_id(0)
    step = pl.program_id(1)
    info = info_ref[...]
    lane = lax.broadcasted_iota(jnp.int32, info.shape, 1)
    first = jnp.where(lane < N_EXPERTS, info, 0.0)
    second = pltpu.roll(jnp.where((lane >= N_EXPERTS) & (lane < 2 * N_EXPERTS), info, 0.0),
                        LANES - N_EXPERTS, 1)
    ind = first + second
    colsum = jnp.sum(ind, axis=0, keepdims=True)

    @pl.when((phase == 0) & (step == 0))
    def _():
        total_ref[...] = jnp.zeros_like(total_ref)

    @pl.when(phase == 0)
    def _():
        total_ref[...] += colsum

    @pl.when((phase == 1) & (step == 0))
    def _():
        cap = jnp.ceil(total_ref[...] * (1.0 / tile)) * tile
        cap8 = jnp.broadcast_to(cap, (8, LANES))
        off_ref[...] = _dot_split_lhs(cap8, ustrict_ref[...], 3)[0:1]
        run_ref[...] = jnp.zeros_like(run_ref)

    @pl.when(phase == 1)
    def _():
        base = _dot(lstrict_ref[...], ind.astype(BF16)) + run_ref[...] + off_ref[...]
        pos1 = jnp.sum(first * base, axis=1, keepdims=True)
        pos2 = jnp.sum(second * base, axis=1, keepdims=True)
        run_ref[...] += colsum
        both = jnp.where(lane == 0, pos1, 0.0) + jnp.where(lane == 1, pos2, 0.0)
        rows = None
        for p in _split(both, 3):
            t = _dot_nt(sel_ref[...], p)
            rows = t if rows is None else rows + t
        pos_ref[...] = rows.astype(jnp.int32)

    cnt_ref[...] = jnp.broadcast_to(total_ref[...], cnt_ref.shape)


def _route(info, tt, tile):
    n_tok = info.shape[0]
    r = np.arange(tt)
    lstrict = jnp.asarray((r[:, None] > r[None, :]).astype(np.float32), BF16)
    l = np.arange(LANES)
    ustrict = jnp.asarray((l[:, None] < l[None, :]).astype(np.float32), BF16)
    sel = jnp.asarray(np.eye(8, LANES, dtype=np.float32), BF16)
    const = lambda p, i: (0, 0)
    return pl.pallas_call(
        functools.partial(_route_kernel, tile=tile),
        grid=(2, n_tok // tt),
        in_specs=[pl.BlockSpec((tt, LANES), lambda p, i: (i, 0)),
                  pl.BlockSpec((tt, tt), const),
                  pl.BlockSpec((LANES, LANES), const),
                  pl.BlockSpec((8, LANES), const)],
        out_specs=[pl.BlockSpec((8, tt), lambda p, i: (0, i * p)),
                   pl.BlockSpec((8, LANES), const)],
        out_shape=[jax.ShapeDtypeStruct((8, n_tok), jnp.int32),
                   jax.ShapeDtypeStruct((8, LANES), F32)],
        scratch_shapes=[pltpu.VMEM((1, LANES), F32)] * 3,
        compiler_params=pltpu.CompilerParams(
            dimension_semantics=("arbitrary", "arbitrary"), vmem_limit_bytes=VMEM_LIMIT),
        name="route",
    )(info, lstrict, ustrict, sel)


def _row_copy(src, src_row, dst, dst_row, sem):
    return pltpu.make_async_copy(src.at[pl.ds(src_row, 1)], dst.at[pl.ds(dst_row, 1)], sem)


def _dispatch_kernel(cnt_ref, off_ref, cap_ref, nu_ref, pos1_ref, pos2_ref, h2_ref, xs_ref,
                     zero_ref, sem):
    step = pl.program_id(0)
    tm = h2_ref.shape[0]
    tile = zero_ref.shape[0]

    @pl.when(step == 0)
    def _():
        zero_ref[...] = jnp.zeros_like(zero_ref)

        def tail_copy(t):
            return pltpu.make_async_copy(zero_ref, xs_ref.at[pl.ds(t * tile, tile)], sem)

        def tail_start(t, c):
            tail_copy(t).start()
            return c

        def tail_wait(t, c):
            tail_copy(t).wait()
            return c

        lax.fori_loop(nu_ref[0], xs_ref.shape[0] // tile, tail_start, 0)
        lax.fori_loop(nu_ref[0], xs_ref.shape[0] // tile, tail_wait, 0)

        def per_expert(e, carry):
            lo = off_ref[e] + cnt_ref[e]
            hi = off_ref[e] + cap_ref[e]

            def start(s, c):
                _row_copy(zero_ref, 0, xs_ref, s, sem).start()
                return c

            def wait(s, c):
                _row_copy(zero_ref, 0, xs_ref, s, sem).wait()
                return c

            lax.fori_loop(lo, hi, start, 0)
            lax.fori_loop(lo, hi, wait, 0)
            return carry

        lax.fori_loop(0, N_EXPERTS, per_expert, 0)

    def start(r, c):
        _row_copy(h2_ref, r, xs_ref, pos1_ref[r], sem).start()
        _row_copy(h2_ref, r, xs_ref, pos2_ref[r], sem).start()
        return c

    def wait(r, c):
        _row_copy(h2_ref, r, xs_ref, pos1_ref[r], sem).wait()
        _row_copy(h2_ref, r, xs_ref, pos2_ref[r], sem).wait()
        return c

    lax.fori_loop(0, tm, start, 0)
    lax.fori_loop(0, tm, wait, 0)


def _dispatch(h2, pos1, pos2, cnt, off, cap, n_used, n_rows, tile, tm):
    n_tok, d = h2.shape
    smem_tok = pl.BlockSpec((tm,), lambda i, *_: (i,), memory_space=pltpu.SMEM)
    return pl.pallas_call(
        _dispatch_kernel,
        grid_spec=pltpu.PrefetchScalarGridSpec(
            num_scalar_prefetch=4,
            grid=(n_tok // tm,),
            in_specs=[smem_tok, smem_tok, pl.BlockSpec((tm, d), lambda i, *_: (i, 0))],
            out_specs=pl.BlockSpec(memory_space=pl.ANY),
            scratch_shapes=[pltpu.VMEM((tile, d), F32), pltpu.SemaphoreType.DMA]),
        out_shape=jax.ShapeDtypeStruct((n_rows, d), F32),
        compiler_params=pltpu.CompilerParams(
            dimension_semantics=("arbitrary",), vmem_limit_bytes=VMEM_LIMIT),
        name="dispatch",
    )(cnt, off, cap, n_used, pos1, pos2, h2)


def _gmm_kernel(te_ref, nu_ref, xs_ref, wg_ref, wu_ref, wd_ref, ys_ref):
    step = pl.program_id(0)

    @pl.when(step < nu_ref[0])
    def _():
        x = xs_ref[...].astype(BF16)
        hid = _silu(_dot(x, wg_ref[...].astype(BF16))) * _dot(x, wu_ref[...].astype(BF16))
        ys_ref[...] = _dot(hid.astype(BF16), wd_ref[...].astype(BF16))

    @pl.when(step >= nu_ref[0])
    def _():
        ys_ref[...] = jnp.zeros_like(ys_ref)


def _gmm(xs, tile_expert, n_used, wg, wu, wd, tile):
    n_rows, d = xs.shape
    de = wg.shape[-1]
    wsel = lambda i, te, nu: (te[i], 0, 0)
    return pl.pallas_call(
        _gmm_kernel,
        grid_spec=pltpu.PrefetchScalarGridSpec(
            num_scalar_prefetch=2,
            grid=(n_rows // tile,),
            in_specs=[pl.BlockSpec((tile, d), lambda i, te, nu: (jnp.where(i < nu[0], i, 0), 0)),
                      pl.BlockSpec((None, d, de), wsel),
                      pl.BlockSpec((None, d, de), wsel),
                      pl.BlockSpec((None, de, d), wsel)],
            out_specs=pl.BlockSpec((tile, d), lambda i, te, nu: (i, 0))),
        out_shape=jax.ShapeDtypeStruct((n_rows, d), F32),
        compiler_params=pltpu.CompilerParams(
            dimension_semantics=("arbitrary",), vmem_limit_bytes=VMEM_LIMIT),
        name="gmm",
    )(tile_expert, n_used, xs, wg, wu, wd)


def _combine_kernel(pos1_ref, pos2_ref, x1_ref, info_ref, gate_ref, ys_ref, out_ref, ybuf, sem):
    tm = x1_ref.shape[0]

    def start(r, c):
        _row_copy(ys_ref, pos1_ref[r], ybuf.at[0], r, sem).start()
        _row_copy(ys_ref, pos2_ref[r], ybuf.at[1], r, sem).start()
        return c

    def wait(r, c):
        _row_copy(ys_ref, pos1_ref[r], ybuf.at[0], r, sem).wait()
        _row_copy(ys_ref, pos2_ref[r], ybuf.at[1], r, sem).wait()
        return c

    lax.fori_loop(0, tm, start, 0)
    lax.fori_loop(0, tm, wait, 0)
    info = info_ref[...]
    lane = lax.broadcasted_iota(jnp.int32, info.shape, 1)
    w1 = jnp.sum(jnp.where(lane == W1_LANE, info, 0.0), axis=1, keepdims=True)
    w2 = jnp.sum(jnp.where(lane == W1_LANE + 1, info, 0.0), axis=1, keepdims=True)
    out_ref[...] = x1_ref[...] + gate_ref[...] * (w1 * ybuf[0] + w2 * ybuf[1])


def _combine(ys, pos1, pos2, x1, info, gate, seq, tm):
    n_tok, d = x1.shape
    smem_tok = pl.BlockSpec((tm,), lambda i: (i,), memory_space=pltpu.SMEM)
    return pl.pallas_call(
        _combine_kernel,
        grid=(n_tok // tm,),
        in_specs=[smem_tok, smem_tok,
                  pl.BlockSpec((tm, d), lambda i: (i, 0)),
                  pl.BlockSpec((tm, LANES), lambda i: (i, 0)),
                  pl.BlockSpec((None, 1, d), lambda i: (i * tm // seq, 0, 0)),
                  pl.BlockSpec(memory_space=pl.ANY)],
        out_specs=pl.BlockSpec((tm, d), lambda i: (i, 0)),
        out_shape=jax.ShapeDtypeStruct((n_tok, d), F32),
        scratch_shapes=[pltpu.VMEM((2, tm, d), F32), pltpu.SemaphoreType.DMA],
        compiler_params=pltpu.CompilerParams(
            dimension_semantics=("arbitrary",), vmem_limit_bytes=VMEM_LIMIT),
        name="combine",
    )(pos1, pos2, x1, info, gate, ys)


def _moe(h2, info, x1, gate, wg, wu, wd):
    bsz, seq, d = x1.shape
    n_tok = bsz * seq
    tile = MOE_TILE
    n_rows = 2 * n_tok + N_EXPERTS * tile
    info = info.reshape(n_tok, LANES)
    pos, cnt = _route(info, min(1024, n_tok), tile)
    cnt = cnt[0, :N_EXPERTS].astype(jnp.int32)
    cap = (cnt + tile - 1) // tile * tile
    end = jnp.cumsum(cap)
    off = end - cap
    tile_start = jnp.arange(n_rows // tile, dtype=jnp.int32) * tile
    tile_expert = jnp.minimum(
        jnp.sum((end[None, :] <= tile_start[:, None]).astype(jnp.int32), axis=1), N_EXPERTS - 1)
    n_used = (end[-1:] // tile).astype(jnp.int32)
    tm = min(512, n_tok)
    xs = _dispatch(h2.reshape(n_tok, d), pos[0], pos[1], cnt, off, cap, n_used, n_rows, tile, tm)
    ys = _gmm(xs, tile_expert, n_used, wg, wu, wd, tile)
    out = _combine(ys, pos[0], pos[1], x1.reshape(n_tok, d), info, gate, seq, tm)
    return out.reshape(bsz, seq, d)


def _layer(x, c_mod, positions, norm_mix, w_in, conv_w, a_log, dt_bias, gdn_out_norm, q_norm,
           k_norm, sinks, w_out, norm_ffn, w_group, b_group, w_router, b_router, w_gate, w_up,
           w_down):
    bsz, seq, d = x.shape
    shift1, scale1, gate1, shift2, scale2, gate2 = [
        m.reshape(bsz, 1, d) for m in jnp.split(c_mod, 6, axis=-1)]

    o = 0
    w_qkv = w_in[:, o:o + 3 * GDN_WIDTH]; o += 3 * GDN_WIDTH
    w_z = w_in[:, o:o + GDN_WIDTH]; o += GDN_WIDTH
    w_ab = w_in[:, o:o + 2 * GDN_HEADS]; o += 2 * GDN_HEADS
    w_sq = w_in[:, o:o + SWA_WIDTH]; o += SWA_WIDTH
    kvw = SWA_KV_HEADS * HEAD_DIM
    w_sk = w_in[:, o:o + kvw]; o += kvw
    w_sv = w_in[:, o:o + kvw]
    rep = lambda w: jnp.repeat(w.reshape(d, SWA_KV_HEADS, 1, HEAD_DIM), SWA_GROUP, axis=2
                               ).reshape(d, SWA_WIDTH)
    w_ab = jnp.zeros((d, LANES), F32).at[:, :2 * GDN_HEADS].set(w_ab)
    tm = min(512, seq)
    qkv, z, ab, sq, sk, sv = _inproj(
        x, shift1, scale1, norm_mix.reshape(1, d), positions, q_norm, k_norm,
        w_qkv.astype(BF16), w_z.astype(BF16),
        w_ab.astype(BF16), w_sq.astype(BF16), rep(w_sk).astype(BF16), rep(w_sv).astype(BF16), tm)

    gdn = _gdn(qkv, z, ab, conv_w, a_log, dt_bias, gdn_out_norm, min(256, seq))
    swa = _swa(sq, sk, sv, sinks)

    padw = lambda w: jnp.zeros((d, LANES), F32).at[:, :w.shape[1]].set(w).astype(BF16)
    padb = lambda b: jnp.zeros((1, LANES), F32).at[0, :b.shape[0]].set(b)
    x1, h2, comb = _outproj(
        gdn, swa, x, gate1, shift2, scale2, norm_ffn.reshape(1, d),
        w_out[:GDN_WIDTH].astype(BF16), w_out[GDN_WIDTH:].astype(BF16),
        padw(w_group), padb(b_group), padw(w_router), padb(b_router), tm)

    return _moe(h2, comb, x1, gate2, w_gate, w_up, w_down)


def kernel(x, c, positions, w_ada, b_ada, norm_mix, w_in, conv_w, a_log, dt_bias, gdn_out_norm,
           q_norm, k_norm, sinks, w_out, norm_ffn, w_group, b_group, w_router, b_router, w_gate,
           w_up, w_down):
    depth = w_ada.shape[0]
    for l in range(depth):
        c_mod = _ada(c, w_ada[l], b_ada[l])
        x = _layer(x, c_mod, positions, norm_mix[l], w_in[l], conv_w[l], a_log[l], dt_bias[l],
                   gdn_out_norm[l], q_norm[l], k_norm[l], sinks[l], w_out[l], norm_ffn[l],
                   w_group[l], b_group[l], w_router[l], b_router[l], w_gate[l], w_up[l],
                   w_down[l])
    return x
```

```python
import functools

import numpy as np
import jax
import jax.numpy as jnp
from jax import lax
from jax.experimental import pallas as pl
from jax.experimental.pallas import tpu as pltpu

F32 = jnp.float32
BF16 = jnp.bfloat16

HEAD_DIM = 64
GDN_HEADS = 8
GDN_WIDTH = GDN_HEADS * HEAD_DIM
CONV_K = 4
CHUNK = 64
SWA_Q_HEADS = 8
SWA_KV_HEADS = 2
SWA_GROUP = SWA_Q_HEADS // SWA_KV_HEADS
SWA_WIDTH = SWA_Q_HEADS * HEAD_DIM
WINDOW = 128
BLOCK = 128
ROPE_THETA = 10000.0
N_GROUPS = 4
EXPERTS_PER_GROUP = 8
N_EXPERTS = N_GROUPS * EXPERTS_PER_GROUP
EPS = 1e-6
LANES = 128
W1_LANE = 2 * N_EXPERTS
MOE_TILE = 256
ROW_UNROLL = 8
HEADS_PER_TILE = 4
INTRA_CHUNKS = 4
TILE = HEADS_PER_TILE * HEAD_DIM
VMEM_LIMIT = 56 * 1024 * 1024


def _dot(a, b):
    return jnp.dot(a, b, preferred_element_type=F32)


def _dot_nt(a, b):
    return lax.dot_general(a, b, (((1,), (1,)), ((), ())), preferred_element_type=F32)


def _dot_tn(a, b):
    return lax.dot_general(a, b, (((0,), (0,)), ((), ())), preferred_element_type=F32)


def _split(x, n):
    parts = []
    r = x
    for i in range(n):
        p = r.astype(BF16)
        parts.append(p)
        if i + 1 < n:
            r = r - p.astype(F32)
    return parts


def _dot_split_lhs(x, w, n):
    acc = None
    for p in _split(x, n):
        t = _dot(p, w)
        acc = t if acc is None else acc + t
    return acc


def _sigmoid(x):
    return 0.5 + 0.5 * jnp.tanh(0.5 * x)


def _silu(x):
    return x * _sigmoid(x)


def _softplus(x):
    return jnp.maximum(x, 0.0) + jnp.log1p(jnp.exp(-jnp.abs(x)))


def _ada_kernel(c_ref, w_ref, b_ref, o_ref):
    act = _silu(c_ref[...])
    o_ref[...] = _dot(act.astype(BF16), w_ref[...].astype(BF16)) + b_ref[...]


def _ada(c, w_ada, b_ada):
    bsz, d = c.shape
    n = w_ada.shape[1]
    rows = 8
    cp = jnp.zeros((rows, d), F32).at[:bsz].set(c)
    tn = 1536
    out = pl.pallas_call(
        _ada_kernel,
        grid=(n // tn,),
        in_specs=[pl.BlockSpec((rows, d), lambda j: (0, 0)),
                  pl.BlockSpec((d, tn), lambda j: (0, j)),
                  pl.BlockSpec((1, tn), lambda j: (0, j))],
        out_specs=pl.BlockSpec((rows, tn), lambda j: (0, j)),
        out_shape=jax.ShapeDtypeStruct((rows, n), F32),
        compiler_params=pltpu.CompilerParams(vmem_limit_bytes=VMEM_LIMIT),
        name="ada",
    )(cp, w_ada, b_ada.reshape(1, n))
    return out[:bsz]


def _inproj_kernel(x_ref, sh_ref, sc_ref, nw_ref, pos_ref, invf_ref, qn_ref, kn_ref, bd512_ref,
                   rep_ref, wqkv_ref, wz_ref, wab_ref, wsq_ref, wsk_ref, wsv_ref,
                   qkv_ref, z_ref, ab_ref, sq_ref, sk_ref, sv_ref):
    x = x_ref[...]
    ms = jnp.mean(x * x, axis=-1, keepdims=True)
    h = x * lax.rsqrt(ms + EPS) * nw_ref[...]
    h = h * (1.0 + sc_ref[...]) + sh_ref[...]
    hb = h.astype(BF16)
    sq = _dot(hb, wsq_ref[...])
    sk = _dot(hb, wsk_ref[...])
    sv = _dot(hb, wsv_ref[...])
    qkv_ref[...] = _dot(hb, wqkv_ref[...]).astype(BF16)
    z_ref[...] = _dot(hb, wz_ref[...]).astype(BF16)
    ab_ref[...] = _dot(hb, wab_ref[...])

    bd512 = bd512_ref[...]
    ang = pos_ref[...].astype(F32) * invf_ref[...]
    cos1 = jnp.cos(ang)
    sin1 = jnp.sin(ang)
    half = HEAD_DIM // 2

    def norm_rope(t, gain):
        width = t.shape[1]
        reps = width // LANES
        cos = jnp.concatenate([cos1] * reps, axis=1)
        sin = jnp.concatenate([sin1] * reps, axis=1)
        first = (lax.broadcasted_iota(jnp.int32, t.shape, 1) % HEAD_DIM) < half
        ms = _dot_split_lhs(t * t, bd512[:width, :width], 2) * (1.0 / HEAD_DIM)
        t = t * lax.rsqrt(ms + EPS) * gain
        swapped = jnp.where(first, pltpu.roll(t, width - half, 1), pltpu.roll(t, half, 1))
        return t * cos + swapped * jnp.where(first, -sin, sin)

    sq_ref[...] = (norm_rope(sq, qn_ref[...]) * (HEAD_DIM ** -0.5)).astype(BF16)
    kn = kn_ref[...]
    sk_ref[...] = _dot(norm_rope(sk, kn[:, :sk.shape[1]]).astype(BF16), rep_ref[...]).astype(BF16)
    sv_ref[...] = _dot(sv.astype(BF16), rep_ref[...]).astype(BF16)


def _inproj(x, shift, scale, norm_w, positions, q_norm, k_norm, wqkv, wz, wab, wsq, wsk, wsv, tm):
    bsz, seq, d = x.shape
    half = HEAD_DIM // 2
    inv_freq = jnp.power(jnp.float32(ROPE_THETA), -jnp.arange(half, dtype=F32) / half)
    invf = jnp.tile(inv_freq, LANES // half).reshape(1, LANES)
    qn = jnp.tile(q_norm, SWA_Q_HEADS).reshape(1, SWA_WIDTH)
    kn = jnp.tile(k_norm, SWA_Q_HEADS).reshape(1, SWA_WIDTH)
    h512 = np.arange(SWA_WIDTH) // HEAD_DIM
    bd512 = jnp.asarray((h512[:, None] == h512[None, :]).astype(np.float32), BF16)
    kv_lane = (h512 // SWA_GROUP) * HEAD_DIM + np.arange(SWA_WIDTH) % HEAD_DIM
    rep = jnp.asarray((np.arange(SWA_KV_HEADS * HEAD_DIM)[:, None] == kv_lane[None, :]
                       ).astype(np.float32), BF16)
    row = lambda b, i: (b, i, 0)
    vec = lambda b, i: (b, 0, 0)
    const = lambda b, i: (0, 0)
    widths = (3 * GDN_WIDTH, GDN_WIDTH, LANES, SWA_WIDTH, SWA_WIDTH, SWA_WIDTH)
    dtypes = (BF16, BF16, F32, BF16, BF16, BF16)
    return pl.pallas_call(
        _inproj_kernel,
        grid=(bsz, seq // tm),
        in_specs=[pl.BlockSpec((None, tm, d), row),
                  pl.BlockSpec((None, 1, d), vec),
                  pl.BlockSpec((None, 1, d), vec),
                  pl.BlockSpec((1, d), const),
                  pl.BlockSpec((None, tm, 1), row)]
                 + [pl.BlockSpec(w.shape, const)
                    for w in (invf, qn, kn, bd512, rep, wqkv, wz, wab, wsq, wsk, wsv)],
        out_specs=[pl.BlockSpec((None, tm, w), row) for w in widths],
        out_shape=[jax.ShapeDtypeStruct((bsz, seq, w), t) for w, t in zip(widths, dtypes)],
        compiler_params=pltpu.CompilerParams(
            dimension_semantics=("parallel", "parallel"), vmem_limit_bytes=VMEM_LIMIT),
        name="inproj",
    )(x, shift, scale, norm_w, positions.reshape(bsz, seq, 1), invf, qn, kn, bd512, rep,
      wqkv, wz, wab, wsq, wsk, wsv)


def _gdn_consts(tb):
    r = np.arange(CHUNK)[:, None]
    cidx = np.arange(TILE)[None, :] % HEAD_DIM
    low = (r >= cidx).astype(np.float32)
    strict = (r > cidx).astype(np.float32)
    eye = (r == cidx).astype(np.float32)
    upper = (r <= cidx).astype(np.float32)
    hb = np.arange(TILE) // HEAD_DIM
    blk = (hb[:, None] == hb[None, :]).astype(np.float32)
    h512 = np.arange(GDN_WIDTH) // HEAD_DIM
    bd512 = (h512[:, None] == h512[None, :]).astype(np.float32)
    e128 = np.zeros((LANES, 2 * GDN_WIDTH), np.float32)
    for j in range(2 * GDN_HEADS):
        e128[j, j * HEAD_DIM:(j + 1) * HEAD_DIM] = 1.0
    t = np.arange(tb)
    lbd = ((t[:, None] >= t[None, :]) & (t[:, None] // CHUNK == t[None, :] // CHUNK))
    return dict(low=jnp.asarray(low), strict=jnp.asarray(strict), eye=jnp.asarray(eye),
                upper=jnp.asarray(upper), blk=jnp.asarray(blk, BF16),
                bd512=jnp.asarray(bd512, BF16), e128=jnp.asarray(e128, BF16),
                lbd=jnp.asarray(lbd.astype(np.float32), BF16))


def _gdn_kernel(xin_ref, z_ref, ab_ref, cw_ref, alog_ref, dt_ref, gamma_ref,
                bd512_ref, e128_ref, lbd_ref, low_ref, strict_ref, eye_ref, upper_ref, blk_ref,
                out_ref,
                tail_ref, s_ref, q_s, k_s, v_s, g_s, b_s, gc_s, o_s, u_s, at_s, kd_s, wq_s, eg_s,
                *, tb):
    step = pl.program_id(1)

    @pl.when(step == 0)
    def _():
        tail_ref[:8, :] = jnp.zeros((8, tail_ref.shape[1]), F32)
        s_ref[...] = jnp.zeros_like(s_ref)

    xf = xin_ref[...].astype(F32)
    tail_ref[8:, :] = xf
    cw = cw_ref[...]
    y = xf * cw[CONV_K - 1:CONV_K]
    for j in range(CONV_K - 1):
        y = y + tail_ref[pl.ds(8 - (CONV_K - 1) + j, tb), :] * cw[j:j + 1]
    tail_ref[:8, :] = xf[tb - 8:]
    y = _silu(y)

    bd512 = bd512_ref[...]
    q = y[:, :GDN_WIDTH]
    k = y[:, GDN_WIDTH:2 * GDN_WIDTH]
    q = q * lax.rsqrt(_dot_split_lhs(q * q, bd512, 2) + EPS) * (HEAD_DIM ** -0.5)
    k = k * lax.rsqrt(_dot_split_lhs(k * k, bd512, 2) + EPS)
    q_s[...] = q
    k_s[...] = k
    v_s[...] = y[:, 2 * GDN_WIDTH:]

    ab = ab_ref[...]
    lane = lax.broadcasted_iota(jnp.int32, ab.shape, 1)
    gb = jnp.where(lane < GDN_HEADS,
                   -jnp.exp(alog_ref[...]) * _softplus(ab + dt_ref[...]),
                   _sigmoid(ab))
    lbd = lbd_ref[...]
    gcum = None
    for p in _split(gb, 3):
        t = _dot(lbd, p)
        gcum = t if gcum is None else gcum + t
    e128 = e128_ref[...]
    gbe = _dot_split_lhs(gb, e128, 3)
    g_s[...] = gbe[:, :GDN_WIDTH]
    b_s[...] = gbe[:, GDN_WIDTH:]
    gc_s[...] = _dot_split_lhs(gcum, e128[:, :GDN_WIDTH], 3)

    low = low_ref[...] > 0.5
    strict = strict_ref[...] > 0.5
    upper = upper_ref[...] > 0.5
    eye = eye_ref[...]
    blk = blk_ref[...]
    blk_f = blk.astype(F32)

    def bdiag(t):
        tb16 = t.astype(BF16)
        return jnp.concatenate([tb16] * HEADS_PER_TILE, axis=0) * blk

    n_tiles = GDN_WIDTH // TILE
    lane_tiles = [slice(g * TILE, (g + 1) * TILE) for g in range(n_tiles)]

    def intra_chunk(units):
        n = range(len(units))
        rows = [pl.ds(c * CHUNK, CHUNK) for c, _ in units]
        sl = [lane_tiles[g] for _, g in units]
        kc = [k_s[rows[i], sl[i]] for i in n]
        qc = [q_s[rows[i], sl[i]] for i in n]
        bc = [b_s[rows[i], sl[i]] for i in n]
        gcc = [gc_s[rows[i], sl[i]] for i in n]
        grow = [jnp.sum(jnp.where(upper, g_s[rows[i], sl[i]], 0.0), axis=0, keepdims=True)
                for i in n]
        decay = [jnp.where(low, jnp.exp(jnp.where(low, gcc[i] - grow[i], 0.0)), 0.0) for i in n]
        kb = [kc[i] * bc[i] for i in n]
        mm = [_dot_nt(jnp.concatenate([kb[i], qc[i]], axis=0).astype(BF16), bdiag(kc[i]))
              for i in n]
        m = [jnp.where(strict, mm[i][:CHUNK] * decay[i], 0.0) for i in n]
        for i in n:
            at_s[rows[i], sl[i]] = (mm[i][CHUNK:] * decay[i]).astype(BF16)

        pw = [_dot(m[i].astype(BF16), bdiag(m[i])) for i in n]
        tinv = [eye - m[i] for i in n]
        for _ in range(4):
            both = [_dot(jnp.concatenate([pw[i], tinv[i]], axis=0).astype(BF16), bdiag(pw[i]))
                    for i in n]
            pw = [both[i][:CHUNK] for i in n]
            tinv = [tinv[i] + both[i][CHUNK:] for i in n]
        last = [_dot(tinv[i].astype(BF16), bdiag(pw[i])) for i in n]
        tinv16 = [(tinv[i] + last[i]).astype(BF16) for i in n]

        eg = [jnp.exp(gcc[i]) for i in n]
        u = [_dot(tinv16[i], bdiag(v_s[rows[i], sl[i]] * bc[i])) for i in n]
        w = [_dot(tinv16[i], bdiag(kb[i] * eg[i])) for i in n]
        for i, (c, g) in enumerate(units):
            glast = gcc[i][CHUNK - 1:CHUNK, :]
            u_s[rows[i], sl[i]] = u[i]
            wq_s[c, :CHUNK, sl[i]] = w[i].astype(BF16)
            wq_s[c, CHUNK:, sl[i]] = (qc[i] * eg[i]).astype(BF16)
            kd_s[rows[i], sl[i]] = (kc[i] * jnp.exp(glast - gcc[i])).astype(BF16)
            eg_s[pl.ds(c * 8, 1), sl[i]] = jnp.exp(glast)

    def inter_chunk(c):
        n = range(n_tiles)
        rows = pl.ds(c * CHUNK, CHUNK)
        state = [s_ref[g] for g in n]
        ws_qs = [_dot(wq_s[c, :, lane_tiles[g]], state[g].astype(BF16)) for g in n]
        vnew = [u_s[rows, lane_tiles[g]] - ws_qs[g][:CHUNK] for g in n]
        intra = [_dot(at_s[rows, lane_tiles[g]], bdiag(vnew[g])) for g in n]
        kv = [_dot_tn(kd_s[rows, lane_tiles[g]], vnew[g].astype(BF16)) for g in n]
        for g in n:
            o_s[rows, lane_tiles[g]] = ws_qs[g][CHUNK:] + intra[g]
            s_ref[g] = state[g] * eg_s[pl.ds(c * 8, 1), lane_tiles[g]] + kv[g] * blk_f

    n_chunks = tb // CHUNK
    for c0 in range(0, n_chunks, INTRA_CHUNKS):
        intra_chunk([(c, g) for c in range(c0, min(c0 + INTRA_CHUNKS, n_chunks))
                     for g in range(n_tiles)])
    for c in range(n_chunks):
        inter_chunk(c)

    o = o_s[...]
    ms = _dot_split_lhs(o * o, bd512, 2) * (1.0 / HEAD_DIM)
    zf = z_ref[...].astype(F32)
    out_ref[...] = (o * lax.rsqrt(ms + EPS) * gamma_ref[...] * _silu(zf)).astype(BF16)


def _gdn(qkv, z, ab, conv_w, a_log, dt_bias, out_norm, tb):
    bsz, seq, _ = qkv.shape
    cs = _gdn_consts(tb)
    pad = lambda v: jnp.zeros((1, LANES), F32).at[0, :GDN_HEADS].set(v)
    gamma = jnp.tile(out_norm, GDN_HEADS).reshape(1, GDN_WIDTH)
    row = lambda b, i: (b, i, 0)
    const = lambda b, i: (0, 0)
    consts = (cs["bd512"], cs["e128"], cs["lbd"], cs["low"], cs["strict"], cs["eye"],
              cs["upper"], cs["blk"])
    small = (conv_w, pad(a_log), pad(dt_bias), gamma)
    wide = pltpu.VMEM((tb, GDN_WIDTH), F32)
    return pl.pallas_call(
        functools.partial(_gdn_kernel, tb=tb),
        grid=(bsz, seq // tb),
        in_specs=[pl.BlockSpec((None, tb, 3 * GDN_WIDTH), row),
                  pl.BlockSpec((None, tb, GDN_WIDTH), row),
                  pl.BlockSpec((None, tb, LANES), row)]
                 + [pl.BlockSpec(a.shape, const) for a in small + consts],
        out_specs=pl.BlockSpec((None, tb, GDN_WIDTH), row),
        out_shape=jax.ShapeDtypeStruct((bsz, seq, GDN_WIDTH), BF16),
        scratch_shapes=[pltpu.VMEM((tb + 8, 3 * GDN_WIDTH), F32),
                        pltpu.VMEM((GDN_WIDTH // TILE, TILE, TILE), F32),
                        wide, wide, wide, wide, wide, wide, wide, wide,
                        pltpu.VMEM((tb, GDN_WIDTH), BF16), pltpu.VMEM((tb, GDN_WIDTH), BF16),
                        pltpu.VMEM((tb // CHUNK, 2 * CHUNK, GDN_WIDTH), BF16),
                        pltpu.VMEM((tb // CHUNK * 8, GDN_WIDTH), F32)],
        compiler_params=pltpu.CompilerParams(
            dimension_semantics=("parallel", "arbitrary"), vmem_limit_bytes=VMEM_LIMIT),
        name="gdn",
    )(qkv, z, ab, *small, *consts)


def _swa_kernel(q_ref, kp_ref, kc_ref, vp_ref, vc_ref, sink_ref, valid_ref, hmask_ref, ones_ref,
                out_ref):
    step = pl.program_id(1)
    q = q_ref[...].astype(F32)
    valid = (valid_ref[...] > 0.5) & ((step > 0) | (valid_ref[...] > 1.5))
    hmask = hmask_ref[...]
    heads = range(SWA_GROUP)
    tiles = range(SWA_KV_HEADS)
    sl = [slice(g * TILE, (g + 1) * TILE) for g in tiles]
    kband = [jnp.concatenate([kp_ref[:, sl[g]], kc_ref[:, sl[g]]], axis=0) for g in tiles]
    vband = [jnp.concatenate([vp_ref[:, sl[g]], vc_ref[:, sl[g]]], axis=0) for g in tiles]
    lhs = [jnp.concatenate([q[:, sl[g]] * hmask[h:h + 1] for h in heads], axis=0).astype(BF16)
           for g in tiles]
    s = [jnp.where(valid, _dot_nt(lhs[g], kband[g]), -jnp.inf) for g in tiles]
    mx = [jnp.maximum(jnp.max(s[g], axis=1, keepdims=True), sink_ref[g]) for g in tiles]
    p = [jnp.exp(s[g] - mx[g]).astype(BF16) for g in tiles]
    psum = [_dot(p[g], ones_ref[...]) for g in tiles]
    pv = [_dot(p[g], vband[g]) for g in tiles]
    for g in tiles:
        rinv = 1.0 / (psum[g] + jnp.exp(sink_ref[g] - mx[g]))
        out = None
        for h in heads:
            rows = slice(h * BLOCK, (h + 1) * BLOCK)
            scale = jnp.concatenate([rinv[rows]] * (TILE // LANES), axis=1) * hmask[h:h + 1]
            term = pv[g][rows] * scale
            out = term if out is None else out + term
        out_ref[:, sl[g]] = out.astype(BF16)


def _swa(sq, sk, sv, sinks):
    bsz, seq, _ = sq.shape
    sink_rows = jnp.repeat(sinks.astype(F32), BLOCK).reshape(SWA_KV_HEADS, SWA_GROUP * BLOCK, 1)
    qi = np.arange(BLOCK)[:, None]
    kj = np.arange(2 * BLOCK)[None, :]
    rel = qi + BLOCK - kj
    inwin = ((rel >= 0) & (rel < WINDOW)).astype(np.float32)
    valid = inwin * np.where(kj >= BLOCK, 2.0, 1.0)
    valid = jnp.asarray(np.tile(valid, (SWA_GROUP, 1)), F32)
    hm = (np.arange(TILE)[None, :] // HEAD_DIM == np.arange(SWA_GROUP)[:, None]).astype(np.float32)
    row = lambda b, i: (b, i, 0)
    prev = lambda b, i: (b, jnp.maximum(i - 1, 0), 0)
    const2 = lambda b, i: (0, 0)
    const3 = lambda b, i: (0, 0, 0)
    blk = (None, BLOCK, SWA_WIDTH)
    return pl.pallas_call(
        _swa_kernel,
        grid=(bsz, seq // BLOCK),
        in_specs=[pl.BlockSpec(blk, row),
                  pl.BlockSpec(blk, prev), pl.BlockSpec(blk, row),
                  pl.BlockSpec(blk, prev), pl.BlockSpec(blk, row),
                  pl.BlockSpec(sink_rows.shape, const3),
                  pl.BlockSpec(valid.shape, const2),
                  pl.BlockSpec(hm.shape, const2),
                  pl.BlockSpec((2 * BLOCK, LANES), const2)],
        out_specs=pl.BlockSpec(blk, row),
        out_shape=jax.ShapeDtypeStruct((bsz, seq, SWA_WIDTH), BF16),
        compiler_params=pltpu.CompilerParams(
            dimension_semantics=("parallel", "parallel"), vmem_limit_bytes=VMEM_LIMIT),
        name="swa",
    )(sq, sk, sk, sv, sv, sink_rows, valid, jnp.asarray(hm), jnp.ones((2 * BLOCK, LANES), BF16))


def _outproj_kernel(gdn_ref, swa_ref, x_ref, gate_ref, sh_ref, sc_ref, nw_ref, wo1_ref, wo2_ref,
                    wgrp_ref, bgrp_ref, wrt_ref, brt_ref, x1_ref, h2_ref, comb_ref):
    mixed = _dot(gdn_ref[...], wo1_ref[...]) + _dot(swa_ref[...], wo2_ref[...])
    x1 = x_ref[...] + gate_ref[...] * mixed
    x1_ref[...] = x1
    ms = jnp.mean(x1 * x1, axis=-1, keepdims=True)
    h2 = x1 * lax.rsqrt(ms + EPS) * nw_ref[...]
    h2 = h2 * (1.0 + sc_ref[...]) + sh_ref[...]
    h2b = h2.astype(BF16)
    h2_ref[...] = h2

    lg = _dot(h2b, wgrp_ref[...]) + bgrp_ref[...]
    lane = lax.broadcasted_iota(jnp.int32, lg.shape, 1).astype(F32)
    none = jnp.float32(LANES)
    lg = jnp.where(lane < N_GROUPS, lg, -jnp.inf)
    gmax = jnp.max(lg, axis=1, keepdims=True)
    gidx = jnp.min(jnp.where(lg == gmax, lane, none), axis=1, keepdims=True)
    p_group = 1.0 / jnp.sum(jnp.exp(lg - gmax), axis=1, keepdims=True)
    le = _dot(h2b, wrt_ref[...]) + brt_ref[...]
    lo = gidx * EXPERTS_PER_GROUP
    in_group = (lane >= lo) & (lane < lo + EXPERTS_PER_GROUP)
    le = jnp.where(in_group, le, -jnp.inf)
    pe = jnp.exp(le - jnp.max(le, axis=1, keepdims=True))
    pe = pe / jnp.sum(pe, axis=1, keepdims=True)
    p1 = jnp.max(pe, axis=1, keepdims=True)
    i1 = jnp.min(jnp.where(in_group & (pe == p1), lane, none), axis=1, keepdims=True)
    rest = in_group & (lane != i1)
    pe2 = jnp.where(rest, pe, -1.0)
    p2 = jnp.max(pe2, axis=1, keepdims=True)
    i2 = jnp.min(jnp.where(rest & (pe2 == p2), lane, none), axis=1, keepdims=True)
    wsum = p1 + p2
    comb_ref[...] = (jnp.where(lane == i1, 1.0, 0.0)
                     + jnp.where(lane == i2 + N_EXPERTS, 1.0, 0.0)
                     + jnp.where(lane == W1_LANE, p1 / wsum * p_group, 0.0)
                     + jnp.where(lane == W1_LANE + 1, p2 / wsum * p_group, 0.0))


def _outproj(gdn, swa, x, gate, shift, scale, norm_w, wo1, wo2, wgrp, bgrp, wrt, brt, tm):
    bsz, seq, d = x.shape
    row = lambda b, i: (b, i, 0)
    vec = lambda b, i: (b, 0, 0)
    const = lambda b, i: (0, 0)
    return pl.pallas_call(
        _outproj_kernel,
        grid=(bsz, seq // tm),
        in_specs=[pl.BlockSpec((None, tm, GDN_WIDTH), row),
                  pl.BlockSpec((None, tm, SWA_WIDTH), row),
                  pl.BlockSpec((None, tm, d), row),
                  pl.BlockSpec((None, 1, d), vec),
                  pl.BlockSpec((None, 1, d), vec),
                  pl.BlockSpec((None, 1, d), vec),
                  pl.BlockSpec((1, d), const)]
                 + [pl.BlockSpec(a.shape, const) for a in (wo1, wo2, wgrp, bgrp, wrt, brt)],
        out_specs=[pl.BlockSpec((None, tm, d), row),
                   pl.BlockSpec((None, tm, d), row),
                   pl.BlockSpec((None, tm, LANES), row)],
        out_shape=[jax.ShapeDtypeStruct((bsz, seq, d), F32),
                   jax.ShapeDtypeStruct((bsz, seq, d), F32),
                   jax.ShapeDtypeStruct((bsz, seq, LANES), F32)],
        compiler_params=pltpu.CompilerParams(
            dimension_semantics=("parallel", "parallel"), vmem_limit_bytes=VMEM_LIMIT),
        name="outproj",
    )(gdn, swa, x, gate, shift, scale, norm_w, wo1, wo2, wgrp, bgrp, wrt, brt)


def _route_kernel(info_ref, lstrict_ref, ustrict_ref, sel_ref, pos_ref, cnt_ref,
                  total_ref, run_ref, off_ref, *, tile):
    phase = pl.program_id(0)
    step = pl.program_id(1)
    info = info_ref[...]
    lane = lax.broadcasted_iota(jnp.int32, info.shape, 1)
    first = jnp.where(lane < N_EXPERTS, info, 0.0)
    second = pltpu.roll(jnp.where((lane >= N_EXPERTS) & (lane < 2 * N_EXPERTS), info, 0.0),
                        LANES - N_EXPERTS, 1)
    ind = first + second
    colsum = jnp.sum(ind, axis=0, keepdims=True)

    @pl.when((phase == 0) & (step == 0))
    def _():
        total_ref[...] = jnp.zeros_like(total_ref)

    @pl.when(phase == 0)
    def _():
        total_ref[...] += colsum

    @pl.when((phase == 1) & (step == 0))
    def _():
        cap = jnp.ceil(total_ref[...] * (1.0 / tile)) * tile
        cap8 = jnp.broadcast_to(cap, (8, LANES))
        off_ref[...] = _dot_split_lhs(cap8, ustrict_ref[...], 3)[0:1]
        run_ref[...] = jnp.zeros_like(run_ref)

    @pl.when(phase == 1)
    def _():
        base = _dot(lstrict_ref[...], ind.astype(BF16)) + run_ref[...] + off_ref[...]
        pos1 = jnp.sum(first * base, axis=1, keepdims=True)
        pos2 = jnp.sum(second * base, axis=1, keepdims=True)
        run_ref[...] += colsum
        both = jnp.where(lane == 0, pos1, 0.0) + jnp.where(lane == 1, pos2, 0.0)
        rows = None
        for p in _split(both, 3):
            t = _dot_nt(sel_ref[...], p)
            rows = t if rows is None else rows + t
        pos_ref[...] = rows.astype(jnp.int32)

    cnt_ref[...] = jnp.broadcast_to(total_ref[...], cnt_ref.shape)


def _route(info, tt, tile):
    n_tok = info.shape[0]
    r = np.arange(tt)
    lstrict = jnp.asarray((r[:, None] > r[None, :]).astype(np.float32), BF16)
    l = np.arange(LANES)
    ustrict = jnp.asarray((l[:, None] < l[None, :]).astype(np.float32), BF16)
    sel = jnp.asarray(np.eye(8, LANES, dtype=np.float32), BF16)
    const = lambda p, i: (0, 0)
    return pl.pallas_call(
        functools.partial(_route_kernel, tile=tile),
        grid=(2, n_tok // tt),
        in_specs=[pl.BlockSpec((tt, LANES), lambda p, i: (i, 0)),
                  pl.BlockSpec((tt, tt), const),
                  pl.BlockSpec((LANES, LANES), const),
                  pl.BlockSpec((8, LANES), const)],
        out_specs=[pl.BlockSpec((8, tt), lambda p, i: (0, i * p)),
                   pl.BlockSpec((8, LANES), const)],
        out_shape=[jax.ShapeDtypeStruct((8, n_tok), jnp.int32),
                   jax.ShapeDtypeStruct((8, LANES), F32)],
        scratch_shapes=[pltpu.VMEM((1, LANES), F32)] * 3,
        compiler_params=pltpu.CompilerParams(
            dimension_semantics=("arbitrary", "arbitrary"), vmem_limit_bytes=VMEM_LIMIT),
        name="route",
    )(info, lstrict, ustrict, sel)


def _row_copy(src, src_row, dst, dst_row, sem):
    return pltpu.make_async_copy(src.at[pl.ds(src_row, 1)], dst.at[pl.ds(dst_row, 1)], sem)


def _for_each_row(n_rows, fn):
    def trip(t, carry):
        for j in range(ROW_UNROLL):
            fn(t * ROW_UNROLL + j)
        return carry

    lax.fori_loop(0, n_rows // ROW_UNROLL, trip, 0)


def _dispatch_kernel(cnt_ref, off_ref, cap_ref, nu_ref, pos1_ref, pos2_ref, h2_ref, xs_ref,
                     zero_ref, sem):
    step = pl.program_id(0)
    tm = h2_ref.shape[0]
    tile = zero_ref.shape[0]

    @pl.when(step == 0)
    def _():
        zero_ref[...] = jnp.zeros_like(zero_ref)

        def tail_copy(t):
            return pltpu.make_async_copy(zero_ref, xs_ref.at[pl.ds(t * tile, tile)], sem)

        def tail_start(t, c):
            tail_copy(t).start()
            return c

        def tail_wait(t, c):
            tail_copy(t).wait()
            return c

        lax.fori_loop(nu_ref[0], xs_ref.shape[0] // tile, tail_start, 0)
        lax.fori_loop(nu_ref[0], xs_ref.shape[0] // tile, tail_wait, 0)

        def per_expert(e, carry):
            lo = off_ref[e] + cnt_ref[e]
            hi = off_ref[e] + cap_ref[e]

            def start(s, c):
                _row_copy(zero_ref, 0, xs_ref, s, sem).start()
                return c

            def wait(s, c):
                _row_copy(zero_ref, 0, xs_ref, s, sem).wait()
                return c

            lax.fori_loop(lo, hi, start, 0)
            lax.fori_loop(lo, hi, wait, 0)
            return carry

        lax.fori_loop(0, N_EXPERTS, per_expert, 0)

    def copies(r):
        return (_row_copy(h2_ref, r, xs_ref, pos1_ref[r], sem),
                _row_copy(h2_ref, r, xs_ref, pos2_ref[r], sem))

    _for_each_row(tm, lambda r: [cp.start(priority=p) for p, cp in enumerate(copies(r))])
    _for_each_row(tm, lambda r: [cp.wait() for cp in copies(r)])


def _dispatch(h2, pos1, pos2, cnt, off, cap, n_used, n_rows, tile, tm):
    n_tok, d = h2.shape
    smem_tok = pl.BlockSpec((tm,), lambda i, *_: (i,), memory_space=pltpu.SMEM)
    return pl.pallas_call(
        _dispatch_kernel,
        grid_spec=pltpu.PrefetchScalarGridSpec(
            num_scalar_prefetch=4,
            grid=(n_tok // tm,),
            in_specs=[smem_tok, smem_tok, pl.BlockSpec((tm, d), lambda i, *_: (i, 0))],
            out_specs=pl.BlockSpec(memory_space=pl.ANY),
            scratch_shapes=[pltpu.VMEM((tile, d), F32), pltpu.SemaphoreType.DMA]),
        out_shape=jax.ShapeDtypeStruct((n_rows, d), F32),
        compiler_params=pltpu.CompilerParams(
            dimension_semantics=("arbitrary",), vmem_limit_bytes=VMEM_LIMIT),
        name="dispatch",
    )(cnt, off, cap, n_used, pos1, pos2, h2)


def _gmm_kernel(te_ref, nu_ref, xs_ref, wg_ref, wu_ref, wd_ref, ys_ref):
    step = pl.program_id(0)

    @pl.when(step < nu_ref[0])
    def _():
        x = xs_ref[...].astype(BF16)
        hid = _silu(_dot(x, wg_ref[...].astype(BF16))) * _dot(x, wu_ref[...].astype(BF16))
        ys_ref[...] = _dot(hid.astype(BF16), wd_ref[...].astype(BF16))

    @pl.when(step >= nu_ref[0])
    def _():
        ys_ref[...] = jnp.zeros_like(ys_ref)


def _gmm(xs, tile_expert, n_used, wg, wu, wd, tile):
    n_rows, d = xs.shape
    de = wg.shape[-1]
    wsel = lambda i, te, nu: (te[i], 0, 0)
    return pl.pallas_call(
        _gmm_kernel,
        grid_spec=pltpu.PrefetchScalarGridSpec(
            num_scalar_prefetch=2,
            grid=(n_rows // tile,),
            in_specs=[pl.BlockSpec((tile, d), lambda i, te, nu: (jnp.where(i < nu[0], i, 0), 0)),
                      pl.BlockSpec((None, d, de), wsel),
                      pl.BlockSpec((None, d, de), wsel),
                      pl.BlockSpec((None, de, d), wsel)],
            out_specs=pl.BlockSpec((tile, d), lambda i, te, nu: (i, 0))),
        out_shape=jax.ShapeDtypeStruct((n_rows, d), F32),
        compiler_params=pltpu.CompilerParams(
            dimension_semantics=("arbitrary",), vmem_limit_bytes=VMEM_LIMIT),
        name="gmm",
    )(tile_expert, n_used, xs, wg, wu, wd)


def _combine_kernel(pos1_ref, pos2_ref, x1_ref, info_ref, gate_ref, ys_ref, out_ref, ybuf, sem):
    tm = x1_ref.shape[0]

    def copies(r):
        return (_row_copy(ys_ref, pos1_ref[r], ybuf.at[0], r, sem),
                _row_copy(ys_ref, pos2_ref[r], ybuf.at[1], r, sem))

    _for_each_row(tm, lambda r: [cp.start(priority=p) for p, cp in enumerate(copies(r))])
    _for_each_row(tm, lambda r: [cp.wait() for cp in copies(r)])
    info = info_ref[...]
    lane = lax.broadcasted_iota(jnp.int32, info.shape, 1)
    w1 = jnp.sum(jnp.where(lane == W1_LANE, info, 0.0), axis=1, keepdims=True)
    w2 = jnp.sum(jnp.where(lane == W1_LANE + 1, info, 0.0), axis=1, keepdims=True)
    out_ref[...] = x1_ref[...] + gate_ref[...] * (w1 * ybuf[0] + w2 * ybuf[1])


def _combine(ys, pos1, pos2, x1, info, gate, seq, tm):
    n_tok, d = x1.shape
    smem_tok = pl.BlockSpec((tm,), lambda i: (i,), memory_space=pltpu.SMEM)
    return pl.pallas_call(
        _combine_kernel,
        grid=(n_tok // tm,),
        in_specs=[smem_tok, smem_tok,
                  pl.BlockSpec((tm, d), lambda i: (i, 0)),
                  pl.BlockSpec((tm, LANES), lambda i: (i, 0)),
                  pl.BlockSpec((None, 1, d), lambda i: (i * tm // seq, 0, 0)),
                  pl.BlockSpec(memory_space=pl.ANY)],
        out_specs=pl.BlockSpec((tm, d), lambda i: (i, 0)),
        out_shape=jax.ShapeDtypeStruct((n_tok, d), F32),
        scratch_shapes=[pltpu.VMEM((2, tm, d), F32), pltpu.SemaphoreType.DMA],
        compiler_params=pltpu.CompilerParams(
            dimension_semantics=("arbitrary",), vmem_limit_bytes=VMEM_LIMIT),
        name="combine",
    )(pos1, pos2, x1, info, gate, ys)


def _moe(h2, info, x1, gate, wg, wu, wd):
    bsz, seq, d = x1.shape
    n_tok = bsz * seq
    tile = MOE_TILE
    n_rows = 2 * n_tok + N_EXPERTS * tile
    info = info.reshape(n_tok, LANES)
    pos, cnt = _route(info, min(1024, n_tok), tile)
    cnt = cnt[0, :N_EXPERTS].astype(jnp.int32)
    cap = (cnt + tile - 1) // tile * tile
    end = jnp.cumsum(cap)
    off = end - cap
    tile_start = jnp.arange(n_rows // tile, dtype=jnp.int32) * tile
    tile_expert = jnp.minimum(
        jnp.sum((end[None, :] <= tile_start[:, None]).astype(jnp.int32), axis=1), N_EXPERTS - 1)
    n_used = (end[-1:] // tile).astype(jnp.int32)
    tm = min(512, n_tok)
    xs = _dispatch(h2.reshape(n_tok, d), pos[0], pos[1], cnt, off, cap, n_used, n_rows, tile, tm)
    ys = _gmm(xs, tile_expert, n_used, wg, wu, wd, tile)
    out = _combine(ys, pos[0], pos[1], x1.reshape(n_tok, d), info, gate, seq, tm)
    return out.reshape(bsz, seq, d)


def _layer(x, c_mod, positions, norm_mix, w_in, conv_w, a_log, dt_bias, gdn_out_norm, q_norm,
           k_norm, sinks, w_out, norm_ffn, w_group, b_group, w_router, b_router, w_gate, w_up,
           w_down):
    bsz, seq, d = x.shape
    shift1, scale1, gate1, shift2, scale2, gate2 = [
        m.reshape(bsz, 1, d) for m in jnp.split(c_mod, 6, axis=-1)]

    o = 0
    w_qkv = w_in[:, o:o + 3 * GDN_WIDTH]; o += 3 * GDN_WIDTH
    w_z = w_in[:, o:o + GDN_WIDTH]; o += GDN_WIDTH
    w_ab = w_in[:, o:o + 2 * GDN_HEADS]; o += 2 * GDN_HEADS
    w_sq = w_in[:, o:o + SWA_WIDTH]; o += SWA_WIDTH
    kvw = SWA_KV_HEADS * HEAD_DIM
    w_sk = w_in[:, o:o + kvw]; o += kvw
    w_sv = w_in[:, o:o + kvw]
    w_ab = jnp.zeros((d, LANES), F32).at[:, :2 * GDN_HEADS].set(w_ab)
    tm = min(512, seq)
    qkv, z, ab, sq, sk, sv = _inproj(
        x, shift1, scale1, norm_mix.reshape(1, d), positions, q_norm, k_norm,
        w_qkv.astype(BF16), w_z.astype(BF16),
        w_ab.astype(BF16), w_sq.astype(BF16), w_sk.astype(BF16), w_sv.astype(BF16), tm)

    gdn = _gdn(qkv, z, ab, conv_w, a_log, dt_bias, gdn_out_norm, min(256, seq))
    swa = _swa(sq, sk, sv, sinks)

    padw = lambda w: jnp.zeros((d, LANES), F32).at[:, :w.shape[1]].set(w).astype(BF16)
    padb = lambda b: jnp.zeros((1, LANES), F32).at[0, :b.shape[0]].set(b)
    x1, h2, comb = _outproj(
        gdn, swa, x, gate1, shift2, scale2, norm_ffn.reshape(1, d),
        w_out[:GDN_WIDTH].astype(BF16), w_out[GDN_WIDTH:].astype(BF16),
        padw(w_group), padb(b_group), padw(w_router), padb(b_router), tm)

    return _moe(h2, comb, x1, gate2, w_gate, w_up, w_down)


def kernel(x, c, positions, w_ada, b_ada, norm_mix, w_in, conv_w, a_log, dt_bias, gdn_out_norm,
           q_norm, k_norm, sinks, w_out, norm_ffn, w_group, b_group, w_router, b_router, w_gate,
           w_up, w_down):
    depth = w_ada.shape[0]
    for l in range(depth):
        c_mod = _ada(c, w_ada[l], b_ada[l])
        x = _layer(x, c_mod, positions, norm_mix[l], w_in[l], conv_w[l], a_log[l], dt_bias[l],
                   gdn_out_norm[l], q_norm[l], k_norm[l], sinks[l], w_out[l], norm_ffn[l],
                   w_group[l], b_group[l], w_router[l], b_router[l], w_gate[l], w_up[l],
                   w_down[l])
    return x
```

```python
import functools

import numpy as np
import jax
import jax.numpy as jnp
from jax import lax
from jax.experimental import pallas as pl
from jax.experimental.pallas import tpu as pltpu

F32 = jnp.float32
BF16 = jnp.bfloat16

HEAD_DIM = 64
GDN_HEADS = 8
GDN_WIDTH = GDN_HEADS * HEAD_DIM
CONV_K = 4
CHUNK = 64
SWA_Q_HEADS = 8
SWA_KV_HEADS = 2
SWA_GROUP = SWA_Q_HEADS // SWA_KV_HEADS
SWA_WIDTH = SWA_Q_HEADS * HEAD_DIM
WINDOW = 128
BLOCK = 128
ROPE_THETA = 10000.0
N_GROUPS = 4
EXPERTS_PER_GROUP = 8
N_EXPERTS = N_GROUPS * EXPERTS_PER_GROUP
EPS = 1e-6
LANES = 128
W1_LANE = 2 * N_EXPERTS
MOE_TILE = 256
ROW_UNROLL = 8
HEADS_PER_TILE = 4
SWA_QBLOCKS = 2
INTRA_CHUNKS = 4
TILE = HEADS_PER_TILE * HEAD_DIM
VMEM_LIMIT = 56 * 1024 * 1024


def _dot(a, b):
    return jnp.dot(a, b, preferred_element_type=F32)


def _dot_nt(a, b):
    return lax.dot_general(a, b, (((1,), (1,)), ((), ())), preferred_element_type=F32)


def _dot_tn(a, b):
    return lax.dot_general(a, b, (((0,), (0,)), ((), ())), preferred_element_type=F32)


def _split(x, n):
    parts = []
    r = x
    for i in range(n):
        p = r.astype(BF16)
        parts.append(p)
        if i + 1 < n:
            r = r - p.astype(F32)
    return parts


def _dot_split_lhs(x, w, n):
    acc = None
    for p in _split(x, n):
        t = _dot(p, w)
        acc = t if acc is None else acc + t
    return acc


def _head_sums(sq, ones_blocks):
    return _dot(sq.astype(BF16), ones_blocks)


def _sigmoid(x):
    return 0.5 + 0.5 * jnp.tanh(0.5 * x)


def _silu(x):
    return x * _sigmoid(x)


def _softplus(x):
    return jnp.maximum(x, 0.0) + jnp.log1p(jnp.exp(-jnp.abs(x)))


def _ada_kernel(c_ref, w_ref, b_ref, o_ref):
    act = _silu(c_ref[...])
    o_ref[...] = _dot(act.astype(BF16), w_ref[...].astype(BF16)) + b_ref[...]


def _ada(c, w_ada, b_ada):
    bsz, d = c.shape
    n = w_ada.shape[1]
    rows = 8
    cp = jnp.zeros((rows, d), F32).at[:bsz].set(c)
    tn = 1536
    out = pl.pallas_call(
        _ada_kernel,
        grid=(n // tn,),
        in_specs=[pl.BlockSpec((rows, d), lambda j: (0, 0)),
                  pl.BlockSpec((d, tn), lambda j: (0, j)),
                  pl.BlockSpec((1, tn), lambda j: (0, j))],
        out_specs=pl.BlockSpec((rows, tn), lambda j: (0, j)),
        out_shape=jax.ShapeDtypeStruct((rows, n), F32),
        compiler_params=pltpu.CompilerParams(vmem_limit_bytes=VMEM_LIMIT),
        name="ada",
    )(cp, w_ada, b_ada.reshape(1, n))
    return out[:bsz]


def _inproj_kernel(x_ref, sh_ref, sc_ref, nw_ref, pos_ref, invf_ref, qn_ref, kn_ref, bd512_ref,
                   rep_ref, cw_ref, wqkv_ref, wz_ref, wab_ref, wsq_ref, wsk_ref, wsv_ref,
                   gq_ref, gk_ref, gv_ref, z_ref, ab_ref, sq_ref, sk_ref, sv_ref, win_ref):
    tm = x_ref.shape[0]

    @pl.when(pl.program_id(1) == 0)
    def _():
        win_ref[:8, :] = jnp.zeros((8, win_ref.shape[1]), F32)

    x = x_ref[...]
    ms = jnp.mean(x * x, axis=-1, keepdims=True)
    h = x * lax.rsqrt(ms + EPS) * nw_ref[...]
    h = h * (1.0 + sc_ref[...]) + sh_ref[...]
    hb = h.astype(BF16)
    sq = _dot(hb, wsq_ref[...])
    sk = _dot(hb, wsk_ref[...])
    sv = _dot(hb, wsv_ref[...])
    win_ref[8:, :] = _dot(hb, wqkv_ref[...])
    z_ref[...] = _dot(hb, wz_ref[...]).astype(BF16)
    ab_ref[...] = _dot(hb, wab_ref[...])
    bd512 = bd512_ref[...]

    cw = cw_ref[...]

    def conv_silu(part):
        cols = slice(part * GDN_WIDTH, (part + 1) * GDN_WIDTH)
        y = None
        for j in range(CONV_K):
            t = win_ref[pl.ds(8 - (CONV_K - 1) + j, tm), cols] * cw[j:j + 1, cols]
            y = t if y is None else y + t
        return _silu(y)

    gq = conv_silu(0)
    gq_ref[...] = (gq * lax.rsqrt(_head_sums(gq * gq, bd512) + EPS)
                   * (HEAD_DIM ** -0.5)).astype(BF16)
    gk = conv_silu(1)
    gk_ref[...] = (gk * lax.rsqrt(_head_sums(gk * gk, bd512) + EPS)).astype(BF16)
    gv_ref[...] = conv_silu(2).astype(BF16)
    win_ref[:8, :] = win_ref[tm:, :]

    ang = pos_ref[...].astype(F32) * invf_ref[...]
    cos1 = jnp.cos(ang)
    sin1 = jnp.sin(ang)
    half = HEAD_DIM // 2

    def norm_rope(t, gain):
        width = t.shape[1]
        reps = width // LANES
        cos = jnp.concatenate([cos1] * reps, axis=1)
        sin = jnp.concatenate([sin1] * reps, axis=1)
        first = (lax.broadcasted_iota(jnp.int32, t.shape, 1) % HEAD_DIM) < half
        ms = _head_sums(t * t, bd512[:width, :width]) * (1.0 / HEAD_DIM)
        t = t * lax.rsqrt(ms + EPS) * gain
        swapped = jnp.where(first, pltpu.roll(t, width - half, 1), pltpu.roll(t, half, 1))
        return t * cos + swapped * jnp.where(first, -sin, sin)

    sq_ref[...] = (norm_rope(sq, qn_ref[...]) * (HEAD_DIM ** -0.5)).astype(BF16)
    kn = kn_ref[...]
    sk_ref[...] = _dot(norm_rope(sk, kn[:, :sk.shape[1]]).astype(BF16), rep_ref[...]).astype(BF16)
    sv_ref[...] = _dot(sv.astype(BF16), rep_ref[...]).astype(BF16)


def _inproj(x, shift, scale, norm_w, positions, q_norm, k_norm, conv_w, wqkv, wz, wab, wsq, wsk, wsv,
            tm):
    bsz, seq, d = x.shape
    half = HEAD_DIM // 2
    inv_freq = jnp.power(jnp.float32(ROPE_THETA), -jnp.arange(half, dtype=F32) / half)
    invf = jnp.tile(inv_freq, LANES // half).reshape(1, LANES)
    qn = jnp.tile(q_norm, SWA_Q_HEADS).reshape(1, SWA_WIDTH)
    kn = jnp.tile(k_norm, SWA_Q_HEADS).reshape(1, SWA_WIDTH)
    h512 = np.arange(SWA_WIDTH) // HEAD_DIM
    bd512 = jnp.asarray((h512[:, None] == h512[None, :]).astype(np.float32), BF16)
    kv_lane = (h512 // SWA_GROUP) * HEAD_DIM + np.arange(SWA_WIDTH) % HEAD_DIM
    rep = jnp.asarray((np.arange(SWA_KV_HEADS * HEAD_DIM)[:, None] == kv_lane[None, :]
                       ).astype(np.float32), BF16)
    row = lambda b, i: (b, i, 0)
    vec = lambda b, i: (b, 0, 0)
    const = lambda b, i: (0, 0)
    widths = (GDN_WIDTH, GDN_WIDTH, GDN_WIDTH, GDN_WIDTH, LANES, SWA_WIDTH, SWA_WIDTH, SWA_WIDTH)
    dtypes = (BF16, BF16, BF16, BF16, F32, BF16, BF16, BF16)
    return pl.pallas_call(
        _inproj_kernel,
        grid=(bsz, seq // tm),
        in_specs=[pl.BlockSpec((None, tm, d), row),
                  pl.BlockSpec((None, 1, d), vec),
                  pl.BlockSpec((None, 1, d), vec),
                  pl.BlockSpec((1, d), const),
                  pl.BlockSpec((None, tm, 1), row)]
                 + [pl.BlockSpec(w.shape, const)
                    for w in (invf, qn, kn, bd512, rep, conv_w, wqkv, wz, wab, wsq, wsk, wsv)],
        out_specs=[pl.BlockSpec((None, tm, w), row) for w in widths],
        out_shape=[jax.ShapeDtypeStruct((bsz, seq, w), t) for w, t in zip(widths, dtypes)],
        scratch_shapes=[pltpu.VMEM((tm + 8, 3 * GDN_WIDTH), F32)],
        compiler_params=pltpu.CompilerParams(
            dimension_semantics=("parallel", "arbitrary"), vmem_limit_bytes=VMEM_LIMIT),
        name="inproj",
    )(x, shift, scale, norm_w, positions.reshape(bsz, seq, 1), invf, qn, kn, bd512, rep, conv_w,
      wqkv, wz, wab, wsq, wsk, wsv)


def _gdn_consts(tb):
    r = np.arange(CHUNK)[:, None]
    cidx = np.arange(TILE)[None, :] % HEAD_DIM
    low = (r >= cidx).astype(np.float32)
    strict = (r > cidx).astype(np.float32)
    eye = (r == cidx).astype(np.float32)
    upper = (r <= cidx).astype(np.float32)
    hb = np.arange(TILE) // HEAD_DIM
    blk = (hb[:, None] == hb[None, :]).astype(np.float32)
    h512 = np.arange(GDN_WIDTH) // HEAD_DIM
    bd512 = (h512[:, None] == h512[None, :]).astype(np.float32)
    e128 = np.zeros((LANES, 2 * GDN_WIDTH), np.float32)
    for j in range(2 * GDN_HEADS):
        e128[j, j * HEAD_DIM:(j + 1) * HEAD_DIM] = 1.0
    t = np.arange(tb)
    lbd = ((t[:, None] >= t[None, :]) & (t[:, None] // CHUNK == t[None, :] // CHUNK))
    return dict(low=jnp.asarray(low), strict=jnp.asarray(strict), eye=jnp.asarray(eye),
                upper=jnp.asarray(upper), blk=jnp.asarray(blk, BF16),
                bd512=jnp.asarray(bd512, BF16), e128=jnp.asarray(e128, BF16),
                lbd=jnp.asarray(lbd.astype(np.float32), BF16))


def _gdn_kernel(q_ref, k_ref, v_ref, z_ref, ab_ref, alog_ref, dt_ref, gamma_ref,
                bd512_ref, e128_ref, lbd_ref, low_ref, strict_ref, eye_ref, upper_ref, blk_ref,
                out_ref,
                s_ref, g_s, b_s, gc_s, o_s, u_s, at_s, kd_s, wq_s, eg_s,
                *, tb, nb):
    step = pl.program_id(0)

    @pl.when(step == 0)
    def _():
        s_ref[...] = jnp.zeros_like(s_ref)

    bd512 = bd512_ref[...]
    low = low_ref[...] > 0.5
    strict = strict_ref[...] > 0.5
    upper = upper_ref[...] > 0.5
    eye = eye_ref[...]
    blk = blk_ref[...]
    blk_f = blk.astype(F32)
    n_tiles = GDN_WIDTH // TILE
    lane_tiles = [slice(g * TILE, (g + 1) * TILE) for g in range(n_tiles)]
    n_chunks = tb // CHUNK

    def prepare(b):
        ab = ab_ref[b]
        lane = lax.broadcasted_iota(jnp.int32, ab.shape, 1)
        gb = jnp.where(lane < GDN_HEADS,
                       -jnp.exp(alog_ref[...]) * _softplus(ab + dt_ref[...]),
                       _sigmoid(ab))
        lbd = lbd_ref[...]
        gcum = None
        for p in _split(gb, 3):
            t = _dot(lbd, p)
            gcum = t if gcum is None else gcum + t
        yield
        e128 = e128_ref[...]
        gbe = _dot_split_lhs(gb, e128, 3)
        g_s[b] = gbe[:, :GDN_WIDTH]
        b_s[b] = gbe[:, GDN_WIDTH:]
        yield
        gc_s[b] = _dot_split_lhs(gcum, e128[:, :GDN_WIDTH], 3)

    def bdiag(t):
        tb16 = t.astype(BF16)
        return jnp.concatenate([tb16] * HEADS_PER_TILE, axis=0) * blk

    def intra_chunk(units, filler):
        n = range(len(units))
        bi = [b for b, _, _ in units]
        rows = [pl.ds(c * CHUNK, CHUNK) for _, c, _ in units]
        sl = [lane_tiles[g] for _, _, g in units]
        kc = [k_ref[bi[i], rows[i], sl[i]].astype(F32) for i in n]
        qc = [q_ref[bi[i], rows[i], sl[i]].astype(F32) for i in n]
        bc = [b_s[bi[i], rows[i], sl[i]] for i in n]
        gcc = [gc_s[bi[i], rows[i], sl[i]] for i in n]
        grow = [jnp.sum(jnp.where(upper, g_s[bi[i], rows[i], sl[i]], 0.0), axis=0, keepdims=True)
                for i in n]
        decay = [jnp.where(low, jnp.exp(jnp.where(low, gcc[i] - grow[i], 0.0)), 0.0) for i in n]
        kb = [kc[i] * bc[i] for i in n]
        mm = [_dot_nt(jnp.concatenate([kb[i], qc[i]], axis=0).astype(BF16), bdiag(kc[i]))
              for i in n]
        next(filler, None)
        m = [jnp.where(strict, mm[i][:CHUNK] * decay[i], 0.0) for i in n]
        for i in n:
            at_s[bi[i], rows[i], sl[i]] = (mm[i][CHUNK:] * decay[i]).astype(BF16)

        pw = [_dot(m[i].astype(BF16), bdiag(m[i])) for i in n]
        tinv = [eye - m[i] for i in n]
        for _ in range(4):
            both = [_dot(jnp.concatenate([pw[i], tinv[i]], axis=0).astype(BF16), bdiag(pw[i]))
                    for i in n]
            next(filler, None)
            pw = [both[i][:CHUNK] for i in n]
            tinv = [tinv[i] + both[i][CHUNK:] for i in n]
        last = [_dot(tinv[i].astype(BF16), bdiag(pw[i])) for i in n]
        next(filler, None)
        tinv16 = [(tinv[i] + last[i]).astype(BF16) for i in n]

        eg = [jnp.exp(gcc[i]) for i in n]
        u = [_dot(tinv16[i], bdiag(v_ref[bi[i], rows[i], sl[i]].astype(F32) * bc[i])) for i in n]
        w = [_dot(tinv16[i], bdiag(kb[i] * eg[i])) for i in n]
        for i, (b, c, g) in enumerate(units):
            glast = gcc[i][CHUNK - 1:CHUNK, :]
            u_s[b, rows[i], sl[i]] = u[i]
            wq_s[b, c, :CHUNK, sl[i]] = w[i].astype(BF16)
            wq_s[b, c, CHUNK:, sl[i]] = (qc[i] * eg[i]).astype(BF16)
            kd_s[b, rows[i], sl[i]] = (kc[i] * jnp.exp(glast - gcc[i])).astype(BF16)
            eg_s[b, pl.ds(c * 8, 1), sl[i]] = jnp.exp(glast)

    def inter_chunk(c):
        chains = [(b, g) for b in range(nb) for g in range(n_tiles)]
        n = range(len(chains))
        rows = pl.ds(c * CHUNK, CHUNK)
        sl = [lane_tiles[g] for _, g in chains]
        bi = [b for b, _ in chains]
        state = [s_ref[b, g] for b, g in chains]
        ws_qs = [_dot(wq_s[bi[i], c, :, sl[i]], state[i].astype(BF16)) for i in n]
        vnew = [u_s[bi[i], rows, sl[i]] - ws_qs[i][:CHUNK] for i in n]
        intra = [_dot(at_s[bi[i], rows, sl[i]], bdiag(vnew[i])) for i in n]
        kv = [_dot_tn(kd_s[bi[i], rows, sl[i]], vnew[i].astype(BF16)) for i in n]
        for i, (b, g) in enumerate(chains):
            o_s[b, rows, sl[i]] = ws_qs[i][CHUNK:] + intra[i]
            s_ref[b, g] = state[i] * eg_s[b, pl.ds(c * 8, 1), sl[i]] + kv[i] * blk_f

    def finish(b):
        o = o_s[b]
        ms = _head_sums(o * o, bd512) * (1.0 / HEAD_DIM)
        zf = z_ref[b].astype(F32)
        out_ref[b] = (o * lax.rsqrt(ms + EPS) * gamma_ref[...] * _silu(zf)).astype(BF16)

    def drain(gen):
        for _ in gen:
            pass

    drain(prepare(0))
    for b in range(nb):
        filler = prepare(b + 1) if b + 1 < nb else iter(())
        for c0 in range(0, n_chunks, INTRA_CHUNKS):
            intra_chunk([(b, c, g) for c in range(c0, min(c0 + INTRA_CHUNKS, n_chunks))
                         for g in range(n_tiles)], filler)
        drain(filler)
    for c in range(n_chunks):
        inter_chunk(c)
    for b in range(nb):
        finish(b)


def _gdn(q, k, v, z, ab, a_log, dt_bias, out_norm, tb):
    nb, seq, _ = q.shape
    cs = _gdn_consts(tb)
    pad = lambda v: jnp.zeros((1, LANES), F32).at[0, :GDN_HEADS].set(v)
    gamma = jnp.tile(out_norm, GDN_HEADS).reshape(1, GDN_WIDTH)
    row = lambda i: (0, i, 0)
    const = lambda i: (0, 0)
    consts = (cs["bd512"], cs["e128"], cs["lbd"], cs["low"], cs["strict"], cs["eye"],
              cs["upper"], cs["blk"])
    small = (pad(a_log), pad(dt_bias), gamma)
    wide = pltpu.VMEM((nb, tb, GDN_WIDTH), F32)
    wide16 = pltpu.VMEM((nb, tb, GDN_WIDTH), BF16)
    n_chunks = tb // CHUNK
    return pl.pallas_call(
        functools.partial(_gdn_kernel, tb=tb, nb=nb),
        grid=(seq // tb,),
        in_specs=[pl.BlockSpec((nb, tb, GDN_WIDTH), row)] * 4
                 + [pl.BlockSpec((nb, tb, LANES), row)]
                 + [pl.BlockSpec(a.shape, const) for a in small + consts],
        out_specs=pl.BlockSpec((nb, tb, GDN_WIDTH), row),
        out_shape=jax.ShapeDtypeStruct((nb, seq, GDN_WIDTH), BF16),
        scratch_shapes=[pltpu.VMEM((nb, GDN_WIDTH // TILE, TILE, TILE), F32),
                        wide, wide, wide, wide, wide,
                        wide16, wide16,
                        pltpu.VMEM((nb, n_chunks, 2 * CHUNK, GDN_WIDTH), BF16),
                        pltpu.VMEM((nb, n_chunks * 8, GDN_WIDTH), F32)],
        compiler_params=pltpu.CompilerParams(
            dimension_semantics=("arbitrary",), vmem_limit_bytes=VMEM_LIMIT),
        name="gdn",
    )(q, k, v, z, ab, *small, *consts)


def _swa_kernel(q_ref, k0_ref, k1_ref, k2_ref, v0_ref, v1_ref, v2_ref, sink_ref, valid_ref,
                hmask_ref, ones_ref, out_ref):
    step = pl.program_id(1)
    hmask = hmask_ref[...]
    heads = range(SWA_GROUP)
    k_blocks = (k0_ref, k1_ref, k2_ref)
    v_blocks = (v0_ref, v1_ref, v2_ref)
    in_window = valid_ref[...] > 0.5
    valid = (in_window & ((step > 0) | (valid_ref[...] > 1.5)), in_window)
    units = [(j, g) for j in range(SWA_QBLOCKS) for g in range(SWA_KV_HEADS)]
    n = range(len(units))
    rows = [slice(j * BLOCK, (j + 1) * BLOCK) for j, _ in units]
    sl = [slice(g * TILE, (g + 1) * TILE) for _, g in units]
    kband = [jnp.concatenate([k_blocks[j][:, sl[i]], k_blocks[j + 1][:, sl[i]]], axis=0)
             for i, (j, _) in enumerate(units)]
    vband = [jnp.concatenate([v_blocks[j][:, sl[i]], v_blocks[j + 1][:, sl[i]]], axis=0)
             for i, (j, _) in enumerate(units)]
    lhs = [jnp.concatenate([q_ref[rows[i], sl[i]].astype(F32) * hmask[h:h + 1] for h in heads],
                           axis=0).astype(BF16) for i in n]
    s = [jnp.where(valid[units[i][0]], _dot_nt(lhs[i], kband[i]), -jnp.inf) for i in n]
    sink = [sink_ref[g] for _, g in units]
    mx = [jnp.maximum(jnp.max(s[i], axis=1, keepdims=True), sink[i]) for i in n]
    p = [jnp.exp(s[i] - mx[i]).astype(BF16) for i in n]
    psum = [_dot(p[i], ones_ref[...]) for i in n]
    pv = [_dot(p[i], vband[i]) for i in n]
    for i in n:
        rinv = 1.0 / (psum[i] + jnp.exp(sink[i] - mx[i]))
        out = None
        for h in heads:
            hrows = slice(h * BLOCK, (h + 1) * BLOCK)
            scale = jnp.concatenate([rinv[hrows]] * (TILE // LANES), axis=1) * hmask[h:h + 1]
            term = pv[i][hrows] * scale
            out = term if out is None else out + term
        out_ref[rows[i], sl[i]] = out.astype(BF16)


def _swa(sq, sk, sv, sinks):
    bsz, seq, _ = sq.shape
    sink_rows = jnp.repeat(sinks.astype(F32), BLOCK).reshape(SWA_KV_HEADS, SWA_GROUP * BLOCK, 1)
    qi = np.arange(BLOCK)[:, None]
    kj = np.arange(2 * BLOCK)[None, :]
    rel = qi + BLOCK - kj
    inwin = ((rel >= 0) & (rel < WINDOW)).astype(np.float32)
    valid = inwin * np.where(kj >= BLOCK, 2.0, 1.0)
    valid = jnp.asarray(np.tile(valid, (SWA_GROUP, 1)), F32)
    hm = (np.arange(TILE)[None, :] // HEAD_DIM == np.arange(SWA_GROUP)[:, None]).astype(np.float32)
    const2 = lambda b, i: (0, 0)
    const3 = lambda b, i: (0, 0, 0)
    blk = (None, BLOCK, SWA_WIDTH)
    key_blocks = [pl.BlockSpec(blk, lambda b, i: (b, jnp.maximum(SWA_QBLOCKS * i - 1, 0), 0)),
                  pl.BlockSpec(blk, lambda b, i: (b, SWA_QBLOCKS * i, 0)),
                  pl.BlockSpec(blk, lambda b, i: (b, SWA_QBLOCKS * i + 1, 0))]
    qrows = (None, SWA_QBLOCKS * BLOCK, SWA_WIDTH)
    return pl.pallas_call(
        _swa_kernel,
        grid=(bsz, seq // (SWA_QBLOCKS * BLOCK)),
        in_specs=[pl.BlockSpec(qrows, lambda b, i: (b, i, 0))] + key_blocks + key_blocks
                 + [pl.BlockSpec(sink_rows.shape, const3),
                    pl.BlockSpec(valid.shape, const2),
                    pl.BlockSpec(hm.shape, const2),
                    pl.BlockSpec((2 * BLOCK, LANES), const2)],
        out_specs=pl.BlockSpec(qrows, lambda b, i: (b, i, 0)),
        out_shape=jax.ShapeDtypeStruct((bsz, seq, SWA_WIDTH), BF16),
        compiler_params=pltpu.CompilerParams(
            dimension_semantics=("parallel", "parallel"), vmem_limit_bytes=VMEM_LIMIT),
        name="swa",
    )(sq, sk, sk, sk, sv, sv, sv, sink_rows, valid, jnp.asarray(hm),
      jnp.ones((2 * BLOCK, LANES), BF16))


def _outproj_kernel(gdn_ref, swa_ref, x_ref, gate_ref, sh_ref, sc_ref, nw_ref, wo1_ref, wo2_ref,
                    wgrp_ref, bgrp_ref, wrt_ref, brt_ref, x1_ref, h2_ref, comb_ref):
    mixed = _dot(gdn_ref[...], wo1_ref[...]) + _dot(swa_ref[...], wo2_ref[...])
    x1 = x_ref[...] + gate_ref[...] * mixed
    x1_ref[...] = x1
    ms = jnp.mean(x1 * x1, axis=-1, keepdims=True)
    h2 = x1 * lax.rsqrt(ms + EPS) * nw_ref[...]
    h2 = h2 * (1.0 + sc_ref[...]) + sh_ref[...]
    h2b = h2.astype(BF16)
    h2_ref[...] = h2

    lg = _dot(h2b, wgrp_ref[...]) + bgrp_ref[...]
    lane = lax.broadcasted_iota(jnp.int32, lg.shape, 1).astype(F32)
    none = jnp.float32(LANES)
    lg = jnp.where(lane < N_GROUPS, lg, -jnp.inf)
    gmax = jnp.max(lg, axis=1, keepdims=True)
    gidx = jnp.min(jnp.where(lg == gmax, lane, none), axis=1, keepdims=True)
    p_group = 1.0 / jnp.sum(jnp.exp(lg - gmax), axis=1, keepdims=True)
    le = _dot(h2b, wrt_ref[...]) + brt_ref[...]
    lo = gidx * EXPERTS_PER_GROUP
    in_group = (lane >= lo) & (lane < lo + EXPERTS_PER_GROUP)
    le = jnp.where(in_group, le, -jnp.inf)
    pe = jnp.exp(le - jnp.max(le, axis=1, keepdims=True))
    pe = pe / jnp.sum(pe, axis=1, keepdims=True)
    p1 = jnp.max(pe, axis=1, keepdims=True)
    i1 = jnp.min(jnp.where(in_group & (pe == p1), lane, none), axis=1, keepdims=True)
    rest = in_group & (lane != i1)
    pe2 = jnp.where(rest, pe, -1.0)
    p2 = jnp.max(pe2, axis=1, keepdims=True)
    i2 = jnp.min(jnp.where(rest & (pe2 == p2), lane, none), axis=1, keepdims=True)
    wsum = p1 + p2
    comb_ref[...] = (jnp.where(lane == i1, 1.0, 0.0)
                     + jnp.where(lane == i2 + N_EXPERTS, 1.0, 0.0)
                     + jnp.where(lane == W1_LANE, p1 / wsum * p_group, 0.0)
                     + jnp.where(lane == W1_LANE + 1, p2 / wsum * p_group, 0.0))


def _outproj(gdn, swa, x, gate, shift, scale, norm_w, wo1, wo2, wgrp, bgrp, wrt, brt, tm):
    bsz, seq, d = x.shape
    row = lambda b, i: (b, i, 0)
    vec = lambda b, i: (b, 0, 0)
    const = lambda b, i: (0, 0)
    return pl.pallas_call(
        _outproj_kernel,
        grid=(bsz, seq // tm),
        in_specs=[pl.BlockSpec((None, tm, GDN_WIDTH), row),
                  pl.BlockSpec((None, tm, SWA_WIDTH), row),
                  pl.BlockSpec((None, tm, d), row),
                  pl.BlockSpec((None, 1, d), vec),
                  pl.BlockSpec((None, 1, d), vec),
                  pl.BlockSpec((None, 1, d), vec),
                  pl.BlockSpec((1, d), const)]
                 + [pl.BlockSpec(a.shape, const) for a in (wo1, wo2, wgrp, bgrp, wrt, brt)],
        out_specs=[pl.BlockSpec((None, tm, d), row),
                   pl.BlockSpec((None, tm, d), row),
                   pl.BlockSpec((None, tm, LANES), row)],
        out_shape=[jax.ShapeDtypeStruct((bsz, seq, d), F32),
                   jax.ShapeDtypeStruct((bsz, seq, d), F32),
                   jax.ShapeDtypeStruct((bsz, seq, LANES), F32)],
        compiler_params=pltpu.CompilerParams(
            dimension_semantics=("parallel", "parallel"), vmem_limit_bytes=VMEM_LIMIT),
        name="outproj",
    )(gdn, swa, x, gate, shift, scale, norm_w, wo1, wo2, wgrp, bgrp, wrt, brt)


def _route_kernel(info_ref, lstrict_ref, ustrict_ref, sel_ref, pos_ref, cnt_ref,
                  total_ref, run_ref, off_ref, *, tile):
    phase = pl.program_id(0)
    step = pl.program_id(1)
    info = info_ref[...]
    lane = lax.broadcasted_iota(jnp.int32, info.shape, 1)
    first = jnp.where(lane < N_EXPERTS, info, 0.0)
    second = pltpu.roll(jnp.where((lane >= N_EXPERTS) & (lane < 2 * N_EXPERTS), info, 0.0),
                        LANES - N_EXPERTS, 1)
    ind = first + second
    colsum = jnp.sum(ind, axis=0, keepdims=True)

    @pl.when((phase == 0) & (step == 0))
    def _():
        total_ref[...] = jnp.zeros_like(total_ref)

    @pl.when(phase == 0)
    def _():
        total_ref[...] += colsum

    @pl.when((phase == 1) & (step == 0))
    def _():
        cap = jnp.ceil(total_ref[...] * (1.0 / tile)) * tile
        cap8 = jnp.broadcast_to(cap, (8, LANES))
        off_ref[...] = _dot_split_lhs(cap8, ustrict_ref[...], 3)[0:1]
        run_ref[...] = jnp.zeros_like(run_ref)

    @pl.when(phase == 1)
    def _():
        base = _dot(lstrict_ref[...], ind.astype(BF16)) + run_ref[...] + off_ref[...]
        pos1 = jnp.sum(first * base, axis=1, keepdims=True)
        pos2 = jnp.sum(second * base, axis=1, keepdims=True)
        run_ref[...] += colsum
        both = jnp.where(lane == 0, pos1, 0.0) + jnp.where(lane == 1, pos2, 0.0)
        rows = None
        for p in _split(both, 3):
            t = _dot_nt(sel_ref[...], p)
            rows = t if rows is None else rows + t
        pos_ref[...] = rows.astype(jnp.int32)

    cnt_ref[...] = jnp.broadcast_to(total_ref[...], cnt_ref.shape)


def _route(info, tt, tile):
    n_tok = info.shape[0]
    r = np.arange(tt)
    lstrict = jnp.asarray((r[:, None] > r[None, :]).astype(np.float32), BF16)
    l = np.arange(LANES)
    ustrict = jnp.asarray((l[:, None] < l[None, :]).astype(np.float32), BF16)
    sel = jnp.asarray(np.eye(8, LANES, dtype=np.float32), BF16)
    const = lambda p, i: (0, 0)
    return pl.pallas_call(
        functools.partial(_route_kernel, tile=tile),
        grid=(2, n_tok // tt),
        in_specs=[pl.BlockSpec((tt, LANES), lambda p, i: (i, 0)),
                  pl.BlockSpec((tt, tt), const),
                  pl.BlockSpec((LANES, LANES), const),
                  pl.BlockSpec((8, LANES), const)],
        out_specs=[pl.BlockSpec((8, tt), lambda p, i: (0, i * p)),
                   pl.BlockSpec((8, LANES), const)],
        out_shape=[jax.ShapeDtypeStruct((8, n_tok), jnp.int32),
                   jax.ShapeDtypeStruct((8, LANES), F32)],
        scratch_shapes=[pltpu.VMEM((1, LANES), F32)] * 3,
        compiler_params=pltpu.CompilerParams(
            dimension_semantics=("arbitrary", "arbitrary"), vmem_limit_bytes=VMEM_LIMIT),
        name="route",
    )(info, lstrict, ustrict, sel)


def _row_copy(src, src_row, dst, dst_row, sem):
    return pltpu.make_async_copy(src.at[pl.ds(src_row, 1)], dst.at[pl.ds(dst_row, 1)], sem)


def _for_each_row(n_rows, fn):
    def trip(t, carry):
        for j in range(ROW_UNROLL):
            fn(t * ROW_UNROLL + j)
        return carry

    lax.fori_loop(0, n_rows // ROW_UNROLL, trip, 0)


def _dispatch_kernel(cnt_ref, off_ref, cap_ref, nu_ref, pos1_ref, pos2_ref, h2_ref, xs_ref,
                     zero_ref, sem):
    step = pl.program_id(0)
    tm = h2_ref.shape[0]
    tile = zero_ref.shape[0]

    @pl.when(step == 0)
    def _():
        zero_ref[...] = jnp.zeros_like(zero_ref)

        def tail_copy(t):
            return pltpu.make_async_copy(zero_ref, xs_ref.at[pl.ds(t * tile, tile)], sem)

        def tail_start(t, c):
            tail_copy(t).start()
            return c

        def tail_wait(t, c):
            tail_copy(t).wait()
            return c

        lax.fori_loop(nu_ref[0], xs_ref.shape[0] // tile, tail_start, 0)
        lax.fori_loop(nu_ref[0], xs_ref.shape[0] // tile, tail_wait, 0)

        def per_expert(e, carry):
            lo = off_ref[e] + cnt_ref[e]
            hi = off_ref[e] + cap_ref[e]

            def start(s, c):
                _row_copy(zero_ref, 0, xs_ref, s, sem).start()
                return c

            def wait(s, c):
                _row_copy(zero_ref, 0, xs_ref, s, sem).wait()
                return c

            lax.fori_loop(lo, hi, start, 0)
            lax.fori_loop(lo, hi, wait, 0)
            return carry

        lax.fori_loop(0, N_EXPERTS, per_expert, 0)

    def copies(r):
        return (_row_copy(h2_ref, r, xs_ref, pos1_ref[r], sem),
                _row_copy(h2_ref, r, xs_ref, pos2_ref[r], sem))

    _for_each_row(tm, lambda r: [cp.start(priority=p) for p, cp in enumerate(copies(r))])
    _for_each_row(tm, lambda r: [cp.wait() for cp in copies(r)])


def _dispatch(h2, pos1, pos2, cnt, off, cap, n_used, n_rows, tile, tm):
    n_tok, d = h2.shape
    smem_tok = pl.BlockSpec((tm,), lambda i, *_: (i,), memory_space=pltpu.SMEM)
    return pl.pallas_call(
        _dispatch_kernel,
        grid_spec=pltpu.PrefetchScalarGridSpec(
            num_scalar_prefetch=4,
            grid=(n_tok // tm,),
            in_specs=[smem_tok, smem_tok, pl.BlockSpec((tm, d), lambda i, *_: (i, 0))],
            out_specs=pl.BlockSpec(memory_space=pl.ANY),
            scratch_shapes=[pltpu.VMEM((tile, d), F32), pltpu.SemaphoreType.DMA]),
        out_shape=jax.ShapeDtypeStruct((n_rows, d), F32),
        compiler_params=pltpu.CompilerParams(
            dimension_semantics=("arbitrary",), vmem_limit_bytes=VMEM_LIMIT),
        name="dispatch",
    )(cnt, off, cap, n_used, pos1, pos2, h2)


def _gmm_kernel(te_ref, nu_ref, xs_ref, wg_ref, wu_ref, wd_ref, ys_ref):
    step = pl.program_id(0)

    @pl.when(step < nu_ref[0])
    def _():
        x = xs_ref[...].astype(BF16)
        hid = _silu(_dot(x, wg_ref[...].astype(BF16))) * _dot(x, wu_ref[...].astype(BF16))
        ys_ref[...] = _dot(hid.astype(BF16), wd_ref[...].astype(BF16))

    @pl.when(step >= nu_ref[0])
    def _():
        ys_ref[...] = jnp.zeros_like(ys_ref)


def _gmm(xs, tile_expert, n_used, wg, wu, wd, tile):
    n_rows, d = xs.shape
    de = wg.shape[-1]
    wsel = lambda i, te, nu: (te[i], 0, 0)
    return pl.pallas_call(
        _gmm_kernel,
        grid_spec=pltpu.PrefetchScalarGridSpec(
            num_scalar_prefetch=2,
            grid=(n_rows // tile,),
            in_specs=[pl.BlockSpec((tile, d), lambda i, te, nu: (jnp.where(i < nu[0], i, 0), 0)),
                      pl.BlockSpec((None, d, de), wsel),
                      pl.BlockSpec((None, d, de), wsel),
                      pl.BlockSpec((None, de, d), wsel)],
            out_specs=pl.BlockSpec((tile, d), lambda i, te, nu: (i, 0))),
        out_shape=jax.ShapeDtypeStruct((n_rows, d), F32),
        compiler_params=pltpu.CompilerParams(
            dimension_semantics=("arbitrary",), vmem_limit_bytes=VMEM_LIMIT),
        name="gmm",
    )(tile_expert, n_used, xs, wg, wu, wd)


def _combine_kernel(pos1_ref, pos2_ref, x1_ref, info_ref, gate_ref, ys_ref, out_ref, ybuf, sem):
    tm = x1_ref.shape[0]

    def copies(r):
        return (_row_copy(ys_ref, pos1_ref[r], ybuf.at[0], r, sem),
                _row_copy(ys_ref, pos2_ref[r], ybuf.at[1], r, sem))

    _for_each_row(tm, lambda r: [cp.start(priority=p) for p, cp in enumerate(copies(r))])
    _for_each_row(tm, lambda r: [cp.wait() for cp in copies(r)])
    info = info_ref[...]
    lane = lax.broadcasted_iota(jnp.int32, info.shape, 1)
    w1 = jnp.sum(jnp.where(lane == W1_LANE, info, 0.0), axis=1, keepdims=True)
    w2 = jnp.sum(jnp.where(lane == W1_LANE + 1, info, 0.0), axis=1, keepdims=True)
    out_ref[...] = x1_ref[...] + gate_ref[...] * (w1 * ybuf[0] + w2 * ybuf[1])


def _combine(ys, pos1, pos2, x1, info, gate, seq, tm):
    n_tok, d = x1.shape
    smem_tok = pl.BlockSpec((tm,), lambda i: (i,), memory_space=pltpu.SMEM)
    return pl.pallas_call(
        _combine_kernel,
        grid=(n_tok // tm,),
        in_specs=[smem_tok, smem_tok,
                  pl.BlockSpec((tm, d), lambda i: (i, 0)),
                  pl.BlockSpec((tm, LANES), lambda i: (i, 0)),
                  pl.BlockSpec((None, 1, d), lambda i: (i * tm // seq, 0, 0)),
                  pl.BlockSpec(memory_space=pl.ANY)],
        out_specs=pl.BlockSpec((tm, d), lambda i: (i, 0)),
        out_shape=jax.ShapeDtypeStruct((n_tok, d), F32),
        scratch_shapes=[pltpu.VMEM((2, tm, d), F32), pltpu.SemaphoreType.DMA],
        compiler_params=pltpu.CompilerParams(
            dimension_semantics=("arbitrary",), vmem_limit_bytes=VMEM_LIMIT),
        name="combine",
    )(pos1, pos2, x1, info, gate, ys)


def _moe(h2, info, x1, gate, wg, wu, wd):
    bsz, seq, d = x1.shape
    n_tok = bsz * seq
    tile = MOE_TILE
    n_rows = 2 * n_tok + N_EXPERTS * tile
    info = info.reshape(n_tok, LANES)
    pos, cnt = _route(info, min(1024, n_tok), tile)
    cnt = cnt[0, :N_EXPERTS].astype(jnp.int32)
    cap = (cnt + tile - 1) // tile * tile
    end = jnp.cumsum(cap)
    off = end - cap
    tile_start = jnp.arange(n_rows // tile, dtype=jnp.int32) * tile
    tile_expert = jnp.minimum(
        jnp.sum((end[None, :] <= tile_start[:, None]).astype(jnp.int32), axis=1), N_EXPERTS - 1)
    n_used = (end[-1:] // tile).astype(jnp.int32)
    tm = min(512, n_tok)
    xs = _dispatch(h2.reshape(n_tok, d), pos[0], pos[1], cnt, off, cap, n_used, n_rows, tile, tm)
    ys = _gmm(xs, tile_expert, n_used, wg, wu, wd, tile)
    out = _combine(ys, pos[0], pos[1], x1.reshape(n_tok, d), info, gate, seq, tm)
    return out.reshape(bsz, seq, d)


def _layer(x, c_mod, positions, norm_mix, w_in, conv_w, a_log, dt_bias, gdn_out_norm, q_norm,
           k_norm, sinks, w_out, norm_ffn, w_group, b_group, w_router, b_router, w_gate, w_up,
           w_down):
    bsz, seq, d = x.shape
    shift1, scale1, gate1, shift2, scale2, gate2 = [
        m.reshape(bsz, 1, d) for m in jnp.split(c_mod, 6, axis=-1)]

    o = 0
    w_qkv = w_in[:, o:o + 3 * GDN_WIDTH]; o += 3 * GDN_WIDTH
    w_z = w_in[:, o:o + GDN_WIDTH]; o += GDN_WIDTH
    w_ab = w_in[:, o:o + 2 * GDN_HEADS]; o += 2 * GDN_HEADS
    w_sq = w_in[:, o:o + SWA_WIDTH]; o += SWA_WIDTH
    kvw = SWA_KV_HEADS * HEAD_DIM
    w_sk = w_in[:, o:o + kvw]; o += kvw
    w_sv = w_in[:, o:o + kvw]
    w_ab = jnp.zeros((d, LANES), F32).at[:, :2 * GDN_HEADS].set(w_ab)
    tm = min(512, seq)
    gq, gk, gv, z, ab, sq, sk, sv = _inproj(
        x, shift1, scale1, norm_mix.reshape(1, d), positions, q_norm, k_norm, conv_w,
        w_qkv.astype(BF16), w_z.astype(BF16),
        w_ab.astype(BF16), w_sq.astype(BF16), w_sk.astype(BF16), w_sv.astype(BF16), tm)

    gdn = _gdn(gq, gk, gv, z, ab, a_log, dt_bias, gdn_out_norm, min(256, seq))
    swa = _swa(sq, sk, sv, sinks)

    padw = lambda w: jnp.zeros((d, LANES), F32).at[:, :w.shape[1]].set(w).astype(BF16)
    padb = lambda b: jnp.zeros((1, LANES), F32).at[0, :b.shape[0]].set(b)
    x1, h2, comb = _outproj(
        gdn, swa, x, gate1, shift2, scale2, norm_ffn.reshape(1, d),
        w_out[:GDN_WIDTH].astype(BF16), w_out[GDN_WIDTH:].astype(BF16),
        padw(w_group), padb(b_group), padw(w_router), padb(b_router), tm)

    return _moe(h2, comb, x1, gate2, w_gate, w_up, w_down)


def kernel(x, c, positions, w_ada, b_ada, norm_mix, w_in, conv_w, a_log, dt_bias, gdn_out_norm,
           q_norm, k_norm, sinks, w_out, norm_ffn, w_group, b_group, w_router, b_router, w_gate,
           w_up, w_down):
    depth = w_ada.shape[0]
    for l in range(depth):
        c_mod = _ada(c, w_ada[l], b_ada[l])
        x = _layer(x, c_mod, positions, norm_mix[l], w_in[l], conv_w[l], a_log[l], dt_bias[l],
                   gdn_out_norm[l], q_norm[l], k_norm[l], sinks[l], w_out[l], norm_ffn[l],
                   w_group[l], b_group[l], w_router[l], b_router[l], w_gate[l], w_up[l],
                   w_down[l])
    return x
```

```python
import functools

import numpy as np
import jax
import jax.numpy as jnp
from jax import lax
from jax.experimental import pallas as pl
from jax.experimental.pallas import tpu as pltpu

F32 = jnp.float32
BF16 = jnp.bfloat16

HEAD_DIM = 64
GDN_HEADS = 8
GDN_WIDTH = GDN_HEADS * HEAD_DIM
CONV_K = 4
CHUNK = 64
SWA_Q_HEADS = 8
SWA_KV_HEADS = 2
SWA_GROUP = SWA_Q_HEADS // SWA_KV_HEADS
SWA_WIDTH = SWA_Q_HEADS * HEAD_DIM
WINDOW = 128
BLOCK = 128
ROPE_THETA = 10000.0
N_GROUPS = 4
EXPERTS_PER_GROUP = 8
N_EXPERTS = N_GROUPS * EXPERTS_PER_GROUP
EPS = 1e-6
LANES = 128
W1_LANE = 2 * N_EXPERTS
MOE_TILE = 256
ROW_UNROLL = 8
HEADS_PER_TILE = 4
SWA_QBLOCKS = 2
INTRA_CHUNKS = 4
TILE = HEADS_PER_TILE * HEAD_DIM
VMEM_LIMIT = 56 * 1024 * 1024


def _dot(a, b):
    return jnp.dot(a, b, preferred_element_type=F32)


def _dot_nt(a, b):
    return lax.dot_general(a, b, (((1,), (1,)), ((), ())), preferred_element_type=F32)


def _dot_tn(a, b):
    return lax.dot_general(a, b, (((0,), (0,)), ((), ())), preferred_element_type=F32)


def _split(x, n):
    parts = []
    r = x
    for i in range(n):
        p = r.astype(BF16)
        parts.append(p)
        if i + 1 < n:
            r = r - p.astype(F32)
    return parts


def _dot_split_lhs(x, w, n):
    acc = None
    for p in _split(x, n):
        t = _dot(p, w)
        acc = t if acc is None else acc + t
    return acc


def _head_sums(sq, ones_blocks):
    return _dot(sq.astype(BF16), ones_blocks)


def _sigmoid(x):
    return 0.5 + 0.5 * jnp.tanh(0.5 * x)


def _silu(x):
    return x * _sigmoid(x)


def _softplus(x):
    return jnp.maximum(x, 0.0) + jnp.log1p(jnp.exp(-jnp.abs(x)))


def _ada_kernel(c_ref, w_ref, b_ref, o_ref):
    act = _silu(c_ref[...])
    o_ref[...] = _dot(act.astype(BF16), w_ref[...].astype(BF16)) + b_ref[...]


def _ada(c, w_ada, b_ada):
    bsz, d = c.shape
    n = w_ada.shape[1]
    rows = 8
    cp = jnp.zeros((rows, d), F32).at[:bsz].set(c)
    tn = 1536
    out = pl.pallas_call(
        _ada_kernel,
        grid=(n // tn,),
        in_specs=[pl.BlockSpec((rows, d), lambda j: (0, 0)),
                  pl.BlockSpec((d, tn), lambda j: (0, j)),
                  pl.BlockSpec((1, tn), lambda j: (0, j))],
        out_specs=pl.BlockSpec((rows, tn), lambda j: (0, j)),
        out_shape=jax.ShapeDtypeStruct((rows, n), F32),
        compiler_params=pltpu.CompilerParams(vmem_limit_bytes=VMEM_LIMIT),
        name="ada",
    )(cp, w_ada, b_ada.reshape(1, n))
    return out[:bsz]


def _inproj_kernel(x_ref, sh_ref, sc_ref, nw_ref, pos_ref, invf_ref, qn_ref, kn_ref, bd512_ref,
                   rep_ref, cw_ref, wqkv_ref, wz_ref, wab_ref, wsq_ref, wsk_ref, wsv_ref,
                   gq_ref, gk_ref, gv_ref, z_ref, ab_ref, sq_ref, sk_ref, sv_ref, win_ref):
    tm = x_ref.shape[0]

    @pl.when(pl.program_id(1) == 0)
    def _():
        win_ref[:8, :] = jnp.zeros((8, win_ref.shape[1]), F32)

    x = x_ref[...]
    ms = jnp.mean(x * x, axis=-1, keepdims=True)
    h = x * lax.rsqrt(ms + EPS) * nw_ref[...]
    h = h * (1.0 + sc_ref[...]) + sh_ref[...]
    hb = h.astype(BF16)
    sq = _dot(hb, wsq_ref[...])
    sk = _dot(hb, wsk_ref[...])
    sv = _dot(hb, wsv_ref[...])
    win_ref[8:, :] = _dot(hb, wqkv_ref[...])
    z_ref[...] = _dot(hb, wz_ref[...]).astype(BF16)
    ab_ref[...] = _dot(hb, wab_ref[...])
    bd512 = bd512_ref[...]

    cw = cw_ref[...]

    def conv_silu(part):
        cols = slice(part * GDN_WIDTH, (part + 1) * GDN_WIDTH)
        y = None
        for j in range(CONV_K):
            t = win_ref[pl.ds(8 - (CONV_K - 1) + j, tm), cols] * cw[j:j + 1, cols]
            y = t if y is None else y + t
        return _silu(y)

    gq = conv_silu(0)
    gq_ref[...] = (gq * lax.rsqrt(_head_sums(gq * gq, bd512) + EPS)
                   * (HEAD_DIM ** -0.5)).astype(BF16)
    gk = conv_silu(1)
    gk_ref[...] = (gk * lax.rsqrt(_head_sums(gk * gk, bd512) + EPS)).astype(BF16)
    gv_ref[...] = conv_silu(2).astype(BF16)
    win_ref[:8, :] = win_ref[tm:, :]

    ang = pos_ref[...].astype(F32) * invf_ref[...]
    cos1 = jnp.cos(ang)
    sin1 = jnp.sin(ang)
    half = HEAD_DIM // 2

    def norm_rope(t, gain):
        width = t.shape[1]
        reps = width // LANES
        cos = jnp.concatenate([cos1] * reps, axis=1)
        sin = jnp.concatenate([sin1] * reps, axis=1)
        first = (lax.broadcasted_iota(jnp.int32, t.shape, 1) % HEAD_DIM) < half
        ms = _head_sums(t * t, bd512[:width, :width]) * (1.0 / HEAD_DIM)
        t = t * lax.rsqrt(ms + EPS) * gain
        swapped = jnp.where(first, pltpu.roll(t, width - half, 1), pltpu.roll(t, half, 1))
        return t * cos + swapped * jnp.where(first, -sin, sin)

    sq_ref[...] = (norm_rope(sq, qn_ref[...]) * (HEAD_DIM ** -0.5)).astype(BF16)
    kn = kn_ref[...]
    sk_ref[...] = _dot(norm_rope(sk, kn[:, :sk.shape[1]]).astype(BF16), rep_ref[...]).astype(BF16)
    sv_ref[...] = _dot(sv.astype(BF16), rep_ref[...]).astype(BF16)


def _inproj(x, shift, scale, norm_w, positions, q_norm, k_norm, conv_w, wqkv, wz, wab, wsq, wsk, wsv,
            tm):
    bsz, seq, d = x.shape
    half = HEAD_DIM // 2
    inv_freq = jnp.power(jnp.float32(ROPE_THETA), -jnp.arange(half, dtype=F32) / half)
    invf = jnp.tile(inv_freq, LANES // half).reshape(1, LANES)
    qn = jnp.tile(q_norm, SWA_Q_HEADS).reshape(1, SWA_WIDTH)
    kn = jnp.tile(k_norm, SWA_Q_HEADS).reshape(1, SWA_WIDTH)
    h512 = np.arange(SWA_WIDTH) // HEAD_DIM
    bd512 = jnp.asarray((h512[:, None] == h512[None, :]).astype(np.float32), BF16)
    kv_lane = (h512 // SWA_GROUP) * HEAD_DIM + np.arange(SWA_WIDTH) % HEAD_DIM
    rep = jnp.asarray((np.arange(SWA_KV_HEADS * HEAD_DIM)[:, None] == kv_lane[None, :]
                       ).astype(np.float32), BF16)
    row = lambda b, i: (b, i, 0)
    vec = lambda b, i: (b, 0, 0)
    const = lambda b, i: (0, 0)
    widths = (GDN_WIDTH, GDN_WIDTH, GDN_WIDTH, GDN_WIDTH, LANES, SWA_WIDTH, SWA_WIDTH, SWA_WIDTH)
    dtypes = (BF16, BF16, BF16, BF16, F32, BF16, BF16, BF16)
    return pl.pallas_call(
        _inproj_kernel,
        grid=(bsz, seq // tm),
        in_specs=[pl.BlockSpec((None, tm, d), row),
                  pl.BlockSpec((None, 1, d), vec),
                  pl.BlockSpec((None, 1, d), vec),
                  pl.BlockSpec((1, d), const),
                  pl.BlockSpec((None, tm, 1), row)]
                 + [pl.BlockSpec(w.shape, const)
                    for w in (invf, qn, kn, bd512, rep, conv_w, wqkv, wz, wab, wsq, wsk, wsv)],
        out_specs=[pl.BlockSpec((None, tm, w), row) for w in widths],
        out_shape=[jax.ShapeDtypeStruct((bsz, seq, w), t) for w, t in zip(widths, dtypes)],
        scratch_shapes=[pltpu.VMEM((tm + 8, 3 * GDN_WIDTH), F32)],
        compiler_params=pltpu.CompilerParams(
            dimension_semantics=("parallel", "arbitrary"), vmem_limit_bytes=VMEM_LIMIT),
        name="inproj",
    )(x, shift, scale, norm_w, positions.reshape(bsz, seq, 1), invf, qn, kn, bd512, rep, conv_w,
      wqkv, wz, wab, wsq, wsk, wsv)


def _gdn_consts(tb):
    r = np.arange(CHUNK)[:, None]
    cidx = np.arange(TILE)[None, :] % HEAD_DIM
    low = (r >= cidx).astype(np.float32)
    strict = (r > cidx).astype(np.float32)
    eye = (r == cidx).astype(np.float32)
    upper = (r <= cidx).astype(np.float32)
    hb = np.arange(TILE) // HEAD_DIM
    blk = (hb[:, None] == hb[None, :]).astype(np.float32)
    h512 = np.arange(GDN_WIDTH) // HEAD_DIM
    bd512 = (h512[:, None] == h512[None, :]).astype(np.float32)
    e128 = np.zeros((LANES, 2 * GDN_WIDTH), np.float32)
    for j in range(2 * GDN_HEADS):
        e128[j, j * HEAD_DIM:(j + 1) * HEAD_DIM] = 1.0
    t = np.arange(tb)
    lbd = ((t[:, None] >= t[None, :]) & (t[:, None] // CHUNK == t[None, :] // CHUNK))
    return dict(low=jnp.asarray(low), strict=jnp.asarray(strict), eye=jnp.asarray(eye),
                upper=jnp.asarray(upper), blk=jnp.asarray(blk, BF16),
                bd512=jnp.asarray(bd512, BF16), e128=jnp.asarray(e128, BF16),
                lbd=jnp.asarray(lbd.astype(np.float32), BF16))


def _gdn_kernel(q_ref, k_ref, v_ref, z_ref, ab_ref, alog_ref, dt_ref, gamma_ref,
                bd512_ref, e128_ref, lbd_ref, low_ref, strict_ref, eye_ref, upper_ref, blk_ref,
                out_ref,
                s_ref, g_s, b_s, gc_s, o_s, u_s, at_s, kd_s, wq_s, eg_s,
                *, tb, nb):
    step = pl.program_id(0)

    @pl.when(step == 0)
    def _():
        s_ref[...] = jnp.zeros_like(s_ref)

    bd512 = bd512_ref[...]
    low = low_ref[...] > 0.5
    strict = strict_ref[...] > 0.5
    upper = upper_ref[...] > 0.5
    eye = eye_ref[...]
    blk = blk_ref[...]
    blk_f = blk.astype(F32)
    n_tiles = GDN_WIDTH // TILE
    lane_tiles = [slice(g * TILE, (g + 1) * TILE) for g in range(n_tiles)]
    n_chunks = tb // CHUNK

    def prepare(b):
        ab = ab_ref[b]
        lane = lax.broadcasted_iota(jnp.int32, ab.shape, 1)
        gb = jnp.where(lane < GDN_HEADS,
                       -jnp.exp(alog_ref[...]) * _softplus(ab + dt_ref[...]),
                       _sigmoid(ab))
        lbd = lbd_ref[...]
        gcum = None
        for p in _split(gb, 3):
            t = _dot(lbd, p)
            gcum = t if gcum is None else gcum + t
        yield
        e128 = e128_ref[...]
        gbe = _dot_split_lhs(gb, e128, 3)
        g_s[b] = gbe[:, :GDN_WIDTH]
        b_s[b] = gbe[:, GDN_WIDTH:]
        yield
        gc_s[b] = _dot_split_lhs(gcum, e128[:, :GDN_WIDTH], 3)

    def bdiag(t):
        tb16 = t.astype(BF16)
        return jnp.concatenate([tb16] * HEADS_PER_TILE, axis=0) * blk

    def intra_chunk(units, filler):
        n = range(len(units))
        bi = [b for b, _, _ in units]
        rows = [pl.ds(c * CHUNK, CHUNK) for _, c, _ in units]
        sl = [lane_tiles[g] for _, _, g in units]
        kc = [k_ref[bi[i], rows[i], sl[i]].astype(F32) for i in n]
        qc = [q_ref[bi[i], rows[i], sl[i]].astype(F32) for i in n]
        bc = [b_s[bi[i], rows[i], sl[i]] for i in n]
        gcc = [gc_s[bi[i], rows[i], sl[i]] for i in n]
        grow = [jnp.sum(jnp.where(upper, g_s[bi[i], rows[i], sl[i]], 0.0), axis=0, keepdims=True)
                for i in n]
        decay = [jnp.where(low, jnp.exp(jnp.where(low, gcc[i] - grow[i], 0.0)), 0.0) for i in n]
        kb = [kc[i] * bc[i] for i in n]
        mm = [_dot_nt(jnp.concatenate([kb[i], qc[i]], axis=0).astype(BF16), bdiag(kc[i]))
              for i in n]
        next(filler, None)
        m = [jnp.where(strict, mm[i][:CHUNK] * decay[i], 0.0) for i in n]
        for i in n:
            at_s[bi[i], rows[i], sl[i]] = (mm[i][CHUNK:] * decay[i]).astype(BF16)

        pw = [_dot(m[i].astype(BF16), bdiag(m[i])) for i in n]
        tinv = [eye - m[i] for i in n]
        for _ in range(4):
            both = [_dot(jnp.concatenate([pw[i], tinv[i]], axis=0).astype(BF16), bdiag(pw[i]))
                    for i in n]
            next(filler, None)
            pw = [both[i][:CHUNK] for i in n]
            tinv = [tinv[i] + both[i][CHUNK:] for i in n]
        last = [_dot(tinv[i].astype(BF16), bdiag(pw[i])) for i in n]
        next(filler, None)
        tinv16 = [(tinv[i] + last[i]).astype(BF16) for i in n]

        eg = [jnp.exp(gcc[i]) for i in n]
        u = [_dot(tinv16[i], bdiag(v_ref[bi[i], rows[i], sl[i]].astype(F32) * bc[i])) for i in n]
        w = [_dot(tinv16[i], bdiag(kb[i] * eg[i])) for i in n]
        for i, (b, c, g) in enumerate(units):
            glast = gcc[i][CHUNK - 1:CHUNK, :]
            u_s[b, rows[i], sl[i]] = u[i]
            wq_s[b, c, :CHUNK, sl[i]] = w[i].astype(BF16)
            wq_s[b, c, CHUNK:, sl[i]] = (qc[i] * eg[i]).astype(BF16)
            kd_s[b, rows[i], sl[i]] = (kc[i] * jnp.exp(glast - gcc[i])).astype(BF16)
            eg_s[b, pl.ds(c * 8, 1), sl[i]] = jnp.exp(glast)

    def inter_chunk(c):
        chains = [(b, g) for b in range(nb) for g in range(n_tiles)]
        n = range(len(chains))
        rows = pl.ds(c * CHUNK, CHUNK)
        sl = [lane_tiles[g] for _, g in chains]
        bi = [b for b, _ in chains]
        state = [s_ref[b, g] for b, g in chains]
        ws_qs = [_dot(wq_s[bi[i], c, :, sl[i]], state[i].astype(BF16)) for i in n]
        vnew = [u_s[bi[i], rows, sl[i]] - ws_qs[i][:CHUNK] for i in n]
        intra = [_dot(at_s[bi[i], rows, sl[i]], bdiag(vnew[i])) for i in n]
        kv = [_dot_tn(kd_s[bi[i], rows, sl[i]], vnew[i].astype(BF16)) for i in n]
        for i, (b, g) in enumerate(chains):
            o_s[b, rows, sl[i]] = ws_qs[i][CHUNK:] + intra[i]
            s_ref[b, g] = state[i] * eg_s[b, pl.ds(c * 8, 1), sl[i]] + kv[i] * blk_f

    def finish(b):
        o = o_s[b]
        ms = _head_sums(o * o, bd512) * (1.0 / HEAD_DIM)
        zf = z_ref[b].astype(F32)
        out_ref[b] = (o * lax.rsqrt(ms + EPS) * gamma_ref[...] * _silu(zf)).astype(BF16)

    def drain(gen):
        for _ in gen:
            pass

    drain(prepare(0))
    for b in range(nb):
        filler = prepare(b + 1) if b + 1 < nb else iter(())
        for c0 in range(0, n_chunks, INTRA_CHUNKS):
            intra_chunk([(b, c, g) for c in range(c0, min(c0 + INTRA_CHUNKS, n_chunks))
                         for g in range(n_tiles)], filler)
        drain(filler)
    for c in range(n_chunks):
        inter_chunk(c)
    for b in range(nb):
        finish(b)


def _gdn(q, k, v, z, ab, a_log, dt_bias, out_norm, tb):
    nb, seq, _ = q.shape
    cs = _gdn_consts(tb)
    pad = lambda v: jnp.zeros((1, LANES), F32).at[0, :GDN_HEADS].set(v)
    gamma = jnp.tile(out_norm, GDN_HEADS).reshape(1, GDN_WIDTH)
    row = lambda i: (0, i, 0)
    const = lambda i: (0, 0)
    consts = (cs["bd512"], cs["e128"], cs["lbd"], cs["low"], cs["strict"], cs["eye"],
              cs["upper"], cs["blk"])
    small = (pad(a_log), pad(dt_bias), gamma)
    wide = pltpu.VMEM((nb, tb, GDN_WIDTH), F32)
    wide16 = pltpu.VMEM((nb, tb, GDN_WIDTH), BF16)
    n_chunks = tb // CHUNK
    return pl.pallas_call(
        functools.partial(_gdn_kernel, tb=tb, nb=nb),
        grid=(seq // tb,),
        in_specs=[pl.BlockSpec((nb, tb, GDN_WIDTH), row)] * 4
                 + [pl.BlockSpec((nb, tb, LANES), row)]
                 + [pl.BlockSpec(a.shape, const) for a in small + consts],
        out_specs=pl.BlockSpec((nb, tb, GDN_WIDTH), row),
        out_shape=jax.ShapeDtypeStruct((nb, seq, GDN_WIDTH), BF16),
        scratch_shapes=[pltpu.VMEM((nb, GDN_WIDTH // TILE, TILE, TILE), F32),
                        wide, wide, wide, wide, wide,
                        wide16, wide16,
                        pltpu.VMEM((nb, n_chunks, 2 * CHUNK, GDN_WIDTH), BF16),
                        pltpu.VMEM((nb, n_chunks * 8, GDN_WIDTH), F32)],
        compiler_params=pltpu.CompilerParams(
            dimension_semantics=("arbitrary",), vmem_limit_bytes=VMEM_LIMIT),
        name="gdn",
    )(q, k, v, z, ab, *small, *consts)


def _swa_kernel(q_ref, k0_ref, k1_ref, k2_ref, v0_ref, v1_ref, v2_ref, sink_ref, valid_ref,
                hmask_ref, ones_ref, out_ref):
    step = pl.program_id(1)
    hmask = hmask_ref[...]
    heads = range(SWA_GROUP)
    k_blocks = (k0_ref, k1_ref, k2_ref)
    v_blocks = (v0_ref, v1_ref, v2_ref)
    in_window = valid_ref[...] > 0.5
    valid = (in_window & ((step > 0) | (valid_ref[...] > 1.5)), in_window)
    units = [(j, g) for j in range(SWA_QBLOCKS) for g in range(SWA_KV_HEADS)]
    n = range(len(units))
    rows = [slice(j * BLOCK, (j + 1) * BLOCK) for j, _ in units]
    sl = [slice(g * TILE, (g + 1) * TILE) for _, g in units]
    kband = [jnp.concatenate([k_blocks[j][:, sl[i]], k_blocks[j + 1][:, sl[i]]], axis=0)
             for i, (j, _) in enumerate(units)]
    vband = [jnp.concatenate([v_blocks[j][:, sl[i]], v_blocks[j + 1][:, sl[i]]], axis=0)
             for i, (j, _) in enumerate(units)]
    lhs = [jnp.concatenate([q_ref[rows[i], sl[i]].astype(F32) * hmask[h:h + 1] for h in heads],
                           axis=0).astype(BF16) for i in n]
    s = [jnp.where(valid[units[i][0]], _dot_nt(lhs[i], kband[i]), -jnp.inf) for i in n]
    sink = [sink_ref[g] for _, g in units]
    mx = [jnp.maximum(jnp.max(s[i], axis=1, keepdims=True), sink[i]) for i in n]
    p = [jnp.exp(s[i] - mx[i]).astype(BF16) for i in n]
    psum = [_dot(p[i], ones_ref[...]) for i in n]
    pv = [_dot(p[i], vband[i]) for i in n]
    for i in n:
        rinv = 1.0 / (psum[i] + jnp.exp(sink[i] - mx[i]))
        out = None
        for h in heads:
            hrows = slice(h * BLOCK, (h + 1) * BLOCK)
            scale = jnp.concatenate([rinv[hrows]] * (TILE // LANES), axis=1) * hmask[h:h + 1]
            term = pv[i][hrows] * scale
            out = term if out is None else out + term
        out_ref[rows[i], sl[i]] = out.astype(BF16)


def _swa(sq, sk, sv, sinks):
    bsz, seq, _ = sq.shape
    sink_rows = jnp.repeat(sinks.astype(F32), BLOCK).reshape(SWA_KV_HEADS, SWA_GROUP * BLOCK, 1)
    qi = np.arange(BLOCK)[:, None]
    kj = np.arange(2 * BLOCK)[None, :]
    rel = qi + BLOCK - kj
    inwin = ((rel >= 0) & (rel < WINDOW)).astype(np.float32)
    valid = inwin * np.where(kj >= BLOCK, 2.0, 1.0)
    valid = jnp.asarray(np.tile(valid, (SWA_GROUP, 1)), F32)
    hm = (np.arange(TILE)[None, :] // HEAD_DIM == np.arange(SWA_GROUP)[:, None]).astype(np.float32)
    const2 = lambda b, i: (0, 0)
    const3 = lambda b, i: (0, 0, 0)
    blk = (None, BLOCK, SWA_WIDTH)
    key_blocks = [pl.BlockSpec(blk, lambda b, i: (b, jnp.maximum(SWA_QBLOCKS * i - 1, 0), 0)),
                  pl.BlockSpec(blk, lambda b, i: (b, SWA_QBLOCKS * i, 0)),
                  pl.BlockSpec(blk, lambda b, i: (b, SWA_QBLOCKS * i + 1, 0))]
    qrows = (None, SWA_QBLOCKS * BLOCK, SWA_WIDTH)
    return pl.pallas_call(
        _swa_kernel,
        grid=(bsz, seq // (SWA_QBLOCKS * BLOCK)),
        in_specs=[pl.BlockSpec(qrows, lambda b, i: (b, i, 0))] + key_blocks + key_blocks
                 + [pl.BlockSpec(sink_rows.shape, const3),
                    pl.BlockSpec(valid.shape, const2),
                    pl.BlockSpec(hm.shape, const2),
                    pl.BlockSpec((2 * BLOCK, LANES), const2)],
        out_specs=pl.BlockSpec(qrows, lambda b, i: (b, i, 0)),
        out_shape=jax.ShapeDtypeStruct((bsz, seq, SWA_WIDTH), BF16),
        compiler_params=pltpu.CompilerParams(
            dimension_semantics=("parallel", "parallel"), vmem_limit_bytes=VMEM_LIMIT),
        name="swa",
    )(sq, sk, sk, sk, sv, sv, sv, sink_rows, valid, jnp.asarray(hm),
      jnp.ones((2 * BLOCK, LANES), BF16))


def _outproj_kernel(gdn_ref, swa_ref, x_ref, gate_ref, sh_ref, sc_ref, nw_ref, wo1_ref, wo2_ref,
                    wgrp_ref, bgrp_ref, wrt_ref, brt_ref, x1_ref, h2_ref, comb_ref):
    mixed = _dot(gdn_ref[...], wo1_ref[...]) + _dot(swa_ref[...], wo2_ref[...])
    x1 = x_ref[...] + gate_ref[...] * mixed
    x1_ref[...] = x1
    ms = jnp.mean(x1 * x1, axis=-1, keepdims=True)
    h2 = x1 * lax.rsqrt(ms + EPS) * nw_ref[...]
    h2 = h2 * (1.0 + sc_ref[...]) + sh_ref[...]
    h2b = h2.astype(BF16)
    h2_ref[...] = h2

    lg = _dot(h2b, wgrp_ref[...]) + bgrp_ref[...]
    lane = lax.broadcasted_iota(jnp.int32, lg.shape, 1).astype(F32)
    none = jnp.float32(LANES)
    lg = jnp.where(lane < N_GROUPS, lg, -jnp.inf)
    gmax = jnp.max(lg, axis=1, keepdims=True)
    gidx = jnp.min(jnp.where(lg == gmax, lane, none), axis=1, keepdims=True)
    p_group = 1.0 / jnp.sum(jnp.exp(lg - gmax), axis=1, keepdims=True)
    le = _dot(h2b, wrt_ref[...]) + brt_ref[...]
    lo = gidx * EXPERTS_PER_GROUP
    in_group = (lane >= lo) & (lane < lo + EXPERTS_PER_GROUP)
    le = jnp.where(in_group, le, -jnp.inf)
    pe = jnp.exp(le - jnp.max(le, axis=1, keepdims=True))
    pe = pe / jnp.sum(pe, axis=1, keepdims=True)
    p1 = jnp.max(pe, axis=1, keepdims=True)
    i1 = jnp.min(jnp.where(in_group & (pe == p1), lane, none), axis=1, keepdims=True)
    rest = in_group & (lane != i1)
    pe2 = jnp.where(rest, pe, -1.0)
    p2 = jnp.max(pe2, axis=1, keepdims=True)
    i2 = jnp.min(jnp.where(rest & (pe2 == p2), lane, none), axis=1, keepdims=True)
    wsum = p1 + p2
    comb_ref[...] = (jnp.where(lane == i1, 1.0, 0.0)
                     + jnp.where(lane == i2 + N_EXPERTS, 1.0, 0.0)
                     + jnp.where(lane == W1_LANE, p1 / wsum * p_group, 0.0)
                     + jnp.where(lane == W1_LANE + 1, p2 / wsum * p_group, 0.0))


def _outproj(gdn, swa, x, gate, shift, scale, norm_w, wo1, wo2, wgrp, bgrp, wrt, brt, tm):
    bsz, seq, d = x.shape
    row = lambda b, i: (b, i, 0)
    vec = lambda b, i: (b, 0, 0)
    const = lambda b, i: (0, 0)
    return pl.pallas_call(
        _outproj_kernel,
        grid=(bsz, seq // tm),
        in_specs=[pl.BlockSpec((None, tm, GDN_WIDTH), row),
                  pl.BlockSpec((None, tm, SWA_WIDTH), row),
                  pl.BlockSpec((None, tm, d), row),
                  pl.BlockSpec((None, 1, d), vec),
                  pl.BlockSpec((None, 1, d), vec),
                  pl.BlockSpec((None, 1, d), vec),
                  pl.BlockSpec((1, d), const)]
                 + [pl.BlockSpec(a.shape, const) for a in (wo1, wo2, wgrp, bgrp, wrt, brt)],
        out_specs=[pl.BlockSpec((None, tm, d), row),
                   pl.BlockSpec((None, tm, d), row),
                   pl.BlockSpec((None, tm, LANES), row)],
        out_shape=[jax.ShapeDtypeStruct((bsz, seq, d), F32),
                   jax.ShapeDtypeStruct((bsz, seq, d), F32),
                   jax.ShapeDtypeStruct((bsz, seq, LANES), F32)],
        compiler_params=pltpu.CompilerParams(
            dimension_semantics=("parallel", "parallel"), vmem_limit_bytes=VMEM_LIMIT),
        name="outproj",
    )(gdn, swa, x, gate, shift, scale, norm_w, wo1, wo2, wgrp, bgrp, wrt, brt)


def _route_kernel(info_ref, lstrict_ref, ustrict_ref, sel_ref, pos_ref, cnt_ref,
                  total_ref, run_ref, off_ref, *, tile):
    phase = pl.program_id(0)
    step = pl.program_id(1)
    info = info_ref[...]
    lane = lax.broadcasted_iota(jnp.int32, info.shape, 1)
    first = jnp.where(lane < N_EXPERTS, info, 0.0)
    second = pltpu.roll(jnp.where((lane >= N_EXPERTS) & (lane < 2 * N_EXPERTS), info, 0.0),
                        LANES - N_EXPERTS, 1)
    ind = first + second
    colsum = jnp.sum(ind, axis=0, keepdims=True)

    @pl.when((phase == 0) & (step == 0))
    def _():
        total_ref[...] = jnp.zeros_like(total_ref)

    @pl.when(phase == 0)
    def _():
        total_ref[...] += colsum

    @pl.when((phase == 1) & (step == 0))
    def _():
        cap = jnp.ceil(total_ref[...] * (1.0 / tile)) * tile
        cap8 = jnp.broadcast_to(cap, (8, LANES))
        off_ref[...] = _dot_split_lhs(cap8, ustrict_ref[...], 3)[0:1]
        run_ref[...] = jnp.zeros_like(run_ref)

    @pl.when(phase == 1)
    def _():
        base = _dot(lstrict_ref[...], ind.astype(BF16)) + run_ref[...] + off_ref[...]
        pos1 = jnp.sum(first * base, axis=1, keepdims=True)
        pos2 = jnp.sum(second * base, axis=1, keepdims=True)
        run_ref[...] += colsum
        both = jnp.where(lane == 0, pos1, 0.0) + jnp.where(lane == 1, pos2, 0.0)
        rows = None
        for p in _split(both, 3):
            t = _dot_nt(sel_ref[...], p)
            rows = t if rows is None else rows + t
        pos_ref[...] = rows.astype(jnp.int32)

    cnt_ref[...] = jnp.broadcast_to(total_ref[...], cnt_ref.shape)


def _route(info, tt, tile):
    n_tok = info.shape[0]
    r = np.arange(tt)
    lstrict = jnp.asarray((r[:, None] > r[None, :]).astype(np.float32), BF16)
    l = np.arange(LANES)
    ustrict = jnp.asarray((l[:, None] < l[None, :]).astype(np.float32), BF16)
    sel = jnp.asarray(np.eye(8, LANES, dtype=np.float32), BF16)
    const = lambda p, i: (0, 0)
    return pl.pallas_call(
        functools.partial(_route_kernel, tile=tile),
        grid=(2, n_tok // tt),
        in_specs=[pl.BlockSpec((tt, LANES), lambda p, i: (i, 0)),
                  pl.BlockSpec((tt, tt), const),
                  pl.BlockSpec((LANES, LANES), const),
                  pl.BlockSpec((8, LANES), const)],
        out_specs=[pl.BlockSpec((8, tt), lambda p, i: (0, i * p)),
                   pl.BlockSpec((8, LANES), const)],
        out_shape=[jax.ShapeDtypeStruct((8, n_tok), jnp.int32),
                   jax.ShapeDtypeStruct((8, LANES), F32)],
        scratch_shapes=[pltpu.VMEM((1, LANES), F32)] * 3,
        compiler_params=pltpu.CompilerParams(
            dimension_semantics=("arbitrary", "arbitrary"), vmem_limit_bytes=VMEM_LIMIT),
        name="route",
    )(info, lstrict, ustrict, sel)


def _row_copy(src, src_row, dst, dst_row, sem):
    return pltpu.make_async_copy(src.at[pl.ds(src_row, 1)], dst.at[pl.ds(dst_row, 1)], sem)


def _for_each_row(n_rows, fn):
    def trip(t, carry):
        for j in range(ROW_UNROLL):
            fn(t * ROW_UNROLL + j)
        return carry

    lax.fori_loop(0, n_rows // ROW_UNROLL, trip, 0)


def _invert_kernel(cnt_ref, off_ref, cap_ref, nu_ref, pos1_ref, pos2_ref, tok_ref, *, tile):
    def point_at_token_0(lo, hi):
        def body(s, carry):
            tok_ref[s] = 0
            return carry

        lax.fori_loop(lo, hi, body, 0)

    def per_expert(e, carry):
        point_at_token_0(off_ref[e] + cnt_ref[e], off_ref[e] + cap_ref[e])
        return carry

    lax.fori_loop(0, N_EXPERTS, per_expert, 0)
    point_at_token_0(nu_ref[0] * tile, tok_ref.shape[0])

    def put(t):
        tok_ref[pos1_ref[t]] = t
        tok_ref[pos2_ref[t]] = t

    _for_each_row(pos1_ref.shape[0], put)


def _invert(pos1, pos2, cnt, off, cap, n_used, n_rows, tile):
    smem = pl.BlockSpec(memory_space=pltpu.SMEM)
    return pl.pallas_call(
        functools.partial(_invert_kernel, tile=tile),
        in_specs=[smem] * 6,
        out_specs=smem,
        out_shape=jax.ShapeDtypeStruct((n_rows,), jnp.int32),
        name="invert",
    )(cnt, off, cap, n_used, pos1, pos2)


def _gmm_kernel(te_ref, nu_ref, tok_ref, h2_ref, wg_ref, wu_ref, wd_ref, ys_ref, xbuf, sem):
    step = pl.program_id(0)
    tile = ys_ref.shape[0]
    last = pl.num_programs(0) - 1

    def gather(t, slot):
        return [_row_copy(h2_ref, tok_ref[t * tile + r], xbuf.at[slot], r, sem.at[slot])
                for r in range(tile)]

    @pl.when(step == 0)
    def _():
        for r, cp in enumerate(gather(0, 0)):
            cp.start(priority=r % 2)
        for cp in gather(0, 0):
            cp.wait()

    @pl.when(step < nu_ref[0])
    def _():
        slot = step % 2
        nxt = jnp.minimum(step + 1, last)
        for r, cp in enumerate(gather(nxt, 1 - slot)):
            cp.start(priority=r % 2)
        x = xbuf[slot].astype(BF16)
        hid = _silu(_dot(x, wg_ref[...].astype(BF16))) * _dot(x, wu_ref[...].astype(BF16))
        ys_ref[...] = _dot(hid.astype(BF16), wd_ref[...].astype(BF16))
        for cp in gather(nxt, 1 - slot):
            cp.wait()

    @pl.when(step >= nu_ref[0])
    def _():
        ys_ref[...] = jnp.zeros_like(ys_ref)


def _gmm(h2, tok, tile_expert, n_used, wg, wu, wd, tile):
    n_rows = tok.shape[0]
    d = h2.shape[1]
    de = wg.shape[-1]
    wsel = lambda i, te, nu, tok: (te[i], 0, 0)
    return pl.pallas_call(
        _gmm_kernel,
        grid_spec=pltpu.PrefetchScalarGridSpec(
            num_scalar_prefetch=3,
            grid=(n_rows // tile,),
            in_specs=[pl.BlockSpec(memory_space=pl.ANY),
                      pl.BlockSpec((None, d, de), wsel),
                      pl.BlockSpec((None, d, de), wsel),
                      pl.BlockSpec((None, de, d), wsel)],
            out_specs=pl.BlockSpec((tile, d), lambda i, te, nu, tok: (i, 0)),
            scratch_shapes=[pltpu.VMEM((2, tile, d), F32), pltpu.SemaphoreType.DMA((2,))]),
        out_shape=jax.ShapeDtypeStruct((n_rows, d), F32),
        compiler_params=pltpu.CompilerParams(
            dimension_semantics=("arbitrary",), vmem_limit_bytes=VMEM_LIMIT),
        name="gmm",
    )(tile_expert, n_used, tok, h2, wg, wu, wd)


def _combine_kernel(pos1_ref, pos2_ref, x1_ref, info_ref, gate_ref, ys_ref, out_ref, ybuf, sem):
    tm = x1_ref.shape[0]

    def copies(r):
        return (_row_copy(ys_ref, pos1_ref[r], ybuf.at[0], r, sem),
                _row_copy(ys_ref, pos2_ref[r], ybuf.at[1], r, sem))

    _for_each_row(tm, lambda r: [cp.start(priority=p) for p, cp in enumerate(copies(r))])
    _for_each_row(tm, lambda r: [cp.wait() for cp in copies(r)])
    info = info_ref[...]
    lane = lax.broadcasted_iota(jnp.int32, info.shape, 1)
    w1 = jnp.sum(jnp.where(lane == W1_LANE, info, 0.0), axis=1, keepdims=True)
    w2 = jnp.sum(jnp.where(lane == W1_LANE + 1, info, 0.0), axis=1, keepdims=True)
    out_ref[...] = x1_ref[...] + gate_ref[...] * (w1 * ybuf[0] + w2 * ybuf[1])


def _combine(ys, pos1, pos2, x1, info, gate, seq, tm):
    n_tok, d = x1.shape
    smem_tok = pl.BlockSpec((tm,), lambda i: (i,), memory_space=pltpu.SMEM)
    return pl.pallas_call(
        _combine_kernel,
        grid=(n_tok // tm,),
        in_specs=[smem_tok, smem_tok,
                  pl.BlockSpec((tm, d), lambda i: (i, 0)),
                  pl.BlockSpec((tm, LANES), lambda i: (i, 0)),
                  pl.BlockSpec((None, 1, d), lambda i: (i * tm // seq, 0, 0)),
                  pl.BlockSpec(memory_space=pl.ANY)],
        out_specs=pl.BlockSpec((tm, d), lambda i: (i, 0)),
        out_shape=jax.ShapeDtypeStruct((n_tok, d), F32),
        scratch_shapes=[pltpu.VMEM((2, tm, d), F32), pltpu.SemaphoreType.DMA],
        compiler_params=pltpu.CompilerParams(
            dimension_semantics=("arbitrary",), vmem_limit_bytes=VMEM_LIMIT),
        name="combine",
    )(pos1, pos2, x1, info, gate, ys)


def _moe(h2, info, x1, gate, wg, wu, wd):
    bsz, seq, d = x1.shape
    n_tok = bsz * seq
    tile = MOE_TILE
    n_rows = 2 * n_tok + N_EXPERTS * tile
    info = info.reshape(n_tok, LANES)
    pos, cnt = _route(info, min(1024, n_tok), tile)
    cnt = cnt[0, :N_EXPERTS].astype(jnp.int32)
    cap = (cnt + tile - 1) // tile * tile
    end = jnp.cumsum(cap)
    off = end - cap
    tile_start = jnp.arange(n_rows // tile, dtype=jnp.int32) * tile
    tile_expert = jnp.minimum(
        jnp.sum((end[None, :] <= tile_start[:, None]).astype(jnp.int32), axis=1), N_EXPERTS - 1)
    n_used = (end[-1:] // tile).astype(jnp.int32)
    tm = min(512, n_tok)
    tok = _invert(pos[0], pos[1], cnt, off, cap, n_used, n_rows, tile)
    ys = _gmm(h2.reshape(n_tok, d), tok, tile_expert, n_used, wg, wu, wd, tile)
    out = _combine(ys, pos[0], pos[1], x1.reshape(n_tok, d), info, gate, seq, tm)
    return out.reshape(bsz, seq, d)


def _layer(x, c_mod, positions, norm_mix, w_in, conv_w, a_log, dt_bias, gdn_out_norm, q_norm,
           k_norm, sinks, w_out, norm_ffn, w_group, b_group, w_router, b_router, w_gate, w_up,
           w_down):
    bsz, seq, d = x.shape
    shift1, scale1, gate1, shift2, scale2, gate2 = [
        m.reshape(bsz, 1, d) for m in jnp.split(c_mod, 6, axis=-1)]

    o = 0
    w_qkv = w_in[:, o:o + 3 * GDN_WIDTH]; o += 3 * GDN_WIDTH
    w_z = w_in[:, o:o + GDN_WIDTH]; o += GDN_WIDTH
    w_ab = w_in[:, o:o + 2 * GDN_HEADS]; o += 2 * GDN_HEADS
    w_sq = w_in[:, o:o + SWA_WIDTH]; o += SWA_WIDTH
    kvw = SWA_KV_HEADS * HEAD_DIM
    w_sk = w_in[:, o:o + kvw]; o += kvw
    w_sv = w_in[:, o:o + kvw]
    w_ab = jnp.zeros((d, LANES), F32).at[:, :2 * GDN_HEADS].set(w_ab)
    tm = min(512, seq)
    gq, gk, gv, z, ab, sq, sk, sv = _inproj(
        x, shift1, scale1, norm_mix.reshape(1, d), positions, q_norm, k_norm, conv_w,
        w_qkv.astype(BF16), w_z.astype(BF16),
        w_ab.astype(BF16), w_sq.astype(BF16), w_sk.astype(BF16), w_sv.astype(BF16), tm)

    gdn = _gdn(gq, gk, gv, z, ab, a_log, dt_bias, gdn_out_norm, min(256, seq))
    swa = _swa(sq, sk, sv, sinks)

    padw = lambda w: jnp.zeros((d, LANES), F32).at[:, :w.shape[1]].set(w).astype(BF16)
    padb = lambda b: jnp.zeros((1, LANES), F32).at[0, :b.shape[0]].set(b)
    x1, h2, comb = _outproj(
        gdn, swa, x, gate1, shift2, scale2, norm_ffn.reshape(1, d),
        w_out[:GDN_WIDTH].astype(BF16), w_out[GDN_WIDTH:].astype(BF16),
        padw(w_group), padb(b_group), padw(w_router), padb(b_router), tm)

    return _moe(h2, comb, x1, gate2, w_gate, w_up, w_down)


def kernel(x, c, positions, w_ada, b_ada, norm_mix, w_in, conv_w, a_log, dt_bias, gdn_out_norm,
           q_norm, k_norm, sinks, w_out, norm_ffn, w_group, b_group, w_router, b_router, w_gate,
           w_up, w_down):
    depth = w_ada.shape[0]
    for l in range(depth):
        c_mod = _ada(c, w_ada[l], b_ada[l])
        x = _layer(x, c_mod, positions, norm_mix[l], w_in[l], conv_w[l], a_log[l], dt_bias[l],
                   gdn_out_norm[l], q_norm[l], k_norm[l], sinks[l], w_out[l], norm_ffn[l],
                   w_group[l], b_group[l], w_router[l], b_router[l], w_gate[l], w_up[l],
                   w_down[l])
    return x
```

```python
import functools

import numpy as np
import jax
import jax.numpy as jnp
from jax import lax
from jax.experimental import pallas as pl
from jax.experimental.pallas import tpu as pltpu

F32 = jnp.float32
BF16 = jnp.bfloat16

HEAD_DIM = 64
GDN_HEADS = 8
GDN_WIDTH = GDN_HEADS * HEAD_DIM
CONV_K = 4
CHUNK = 64
SWA_Q_HEADS = 8
SWA_KV_HEADS = 2
SWA_GROUP = SWA_Q_HEADS // SWA_KV_HEADS
SWA_WIDTH = SWA_Q_HEADS * HEAD_DIM
WINDOW = 128
BLOCK = 128
ROPE_THETA = 10000.0
N_GROUPS = 4
EXPERTS_PER_GROUP = 8
N_EXPERTS = N_GROUPS * EXPERTS_PER_GROUP
EPS = 1e-6
LANES = 128
W1_LANE = 2 * N_EXPERTS
MOE_TILE = 256
ROW_UNROLL = 8
HEADS_PER_TILE = 4
SWA_QBLOCKS = 2
INTRA_CHUNKS = 4
TILE = HEADS_PER_TILE * HEAD_DIM
VMEM_LIMIT = 56 * 1024 * 1024


def _dot(a, b):
    return jnp.dot(a, b, preferred_element_type=F32)


def _dot_nt(a, b):
    return lax.dot_general(a, b, (((1,), (1,)), ((), ())), preferred_element_type=F32)


def _dot_tn(a, b):
    return lax.dot_general(a, b, (((0,), (0,)), ((), ())), preferred_element_type=F32)


def _split(x, n):
    parts = []
    r = x
    for i in range(n):
        p = r.astype(BF16)
        parts.append(p)
        if i + 1 < n:
            r = r - p.astype(F32)
    return parts


def _dot_split_lhs(x, w, n):
    acc = None
    for p in _split(x, n):
        t = _dot(p, w)
        acc = t if acc is None else acc + t
    return acc


def _head_sums(sq, ones_blocks):
    return _dot(sq.astype(BF16), ones_blocks)


def _sigmoid(x):
    return 0.5 + 0.5 * jnp.tanh(0.5 * x)


def _silu(x):
    return x * _sigmoid(x)


def _softplus(x):
    return jnp.maximum(x, 0.0) + jnp.log1p(jnp.exp(-jnp.abs(x)))


def _ada_kernel(c_ref, w_ref, b_ref, o_ref):
    act = _silu(c_ref[...])
    o_ref[...] = _dot(act.astype(BF16), w_ref[...].astype(BF16)) + b_ref[...]


def _ada(c, w_ada, b_ada):
    bsz, d = c.shape
    n = w_ada.shape[1]
    rows = 8
    cp = jnp.zeros((rows, d), F32).at[:bsz].set(c)
    tn = 1536
    out = pl.pallas_call(
        _ada_kernel,
        grid=(n // tn,),
        in_specs=[pl.BlockSpec((rows, d), lambda j: (0, 0)),
                  pl.BlockSpec((d, tn), lambda j: (0, j)),
                  pl.BlockSpec((1, tn), lambda j: (0, j))],
        out_specs=pl.BlockSpec((rows, tn), lambda j: (0, j)),
        out_shape=jax.ShapeDtypeStruct((rows, n), F32),
        compiler_params=pltpu.CompilerParams(vmem_limit_bytes=VMEM_LIMIT),
        name="ada",
    )(cp, w_ada, b_ada.reshape(1, n))
    return out[:bsz]


def _inproj_kernel(x_ref, sh_ref, sc_ref, nw_ref, pos_ref, invf_ref, qn_ref, kn_ref, bd512_ref,
                   rep_ref, cw_ref, wqkv_ref, wz_ref, wab_ref, wsq_ref, wsk_ref, wsv_ref,
                   gq_ref, gk_ref, gv_ref, z_ref, ab_ref, sq_ref, sk_ref, sv_ref, win_ref):
    tm = x_ref.shape[0]

    @pl.when(pl.program_id(1) == 0)
    def _():
        win_ref[:8, :] = jnp.zeros((8, win_ref.shape[1]), F32)

    x = x_ref[...]
    ms = jnp.mean(x * x, axis=-1, keepdims=True)
    h = x * lax.rsqrt(ms + EPS) * nw_ref[...]
    h = h * (1.0 + sc_ref[...]) + sh_ref[...]
    hb = h.astype(BF16)
    sq = _dot(hb, wsq_ref[...])
    sk = _dot(hb, wsk_ref[...])
    sv = _dot(hb, wsv_ref[...])
    win_ref[8:, :] = _dot(hb, wqkv_ref[...])
    z_ref[...] = _dot(hb, wz_ref[...]).astype(BF16)
    ab_ref[...] = _dot(hb, wab_ref[...])
    bd512 = bd512_ref[...]

    cw = cw_ref[...]

    def conv_silu(part):
        cols = slice(part * GDN_WIDTH, (part + 1) * GDN_WIDTH)
        y = None
        for j in range(CONV_K):
            t = win_ref[pl.ds(8 - (CONV_K - 1) + j, tm), cols] * cw[j:j + 1, cols]
            y = t if y is None else y + t
        return _silu(y)

    gq = conv_silu(0)
    gq_ref[...] = (gq * lax.rsqrt(_head_sums(gq * gq, bd512) + EPS)
                   * (HEAD_DIM ** -0.5)).astype(BF16)
    gk = conv_silu(1)
    gk_ref[...] = (gk * lax.rsqrt(_head_sums(gk * gk, bd512) + EPS)).astype(BF16)
    gv_ref[...] = conv_silu(2).astype(BF16)
    win_ref[:8, :] = win_ref[tm:, :]

    ang = pos_ref[...].astype(F32) * invf_ref[...]
    cos1 = jnp.cos(ang)
    sin1 = jnp.sin(ang)
    half = HEAD_DIM // 2

    def norm_rope(t, gain):
        width = t.shape[1]
        reps = width // LANES
        cos = jnp.concatenate([cos1] * reps, axis=1)
        sin = jnp.concatenate([sin1] * reps, axis=1)
        first = (lax.broadcasted_iota(jnp.int32, t.shape, 1) % HEAD_DIM) < half
        ms = _head_sums(t * t, bd512[:width, :width]) * (1.0 / HEAD_DIM)
        t = t * lax.rsqrt(ms + EPS) * gain
        swapped = jnp.where(first, pltpu.roll(t, width - half, 1), pltpu.roll(t, half, 1))
        return t * cos + swapped * jnp.where(first, -sin, sin)

    sq_ref[...] = (norm_rope(sq, qn_ref[...]) * (HEAD_DIM ** -0.5)).astype(BF16)
    kn = kn_ref[...]
    sk_ref[...] = _dot(norm_rope(sk, kn[:, :sk.shape[1]]).astype(BF16), rep_ref[...]).astype(BF16)
    sv_ref[...] = _dot(sv.astype(BF16), rep_ref[...]).astype(BF16)


def _inproj(x, shift, scale, norm_w, positions, q_norm, k_norm, conv_w, wqkv, wz, wab, wsq, wsk, wsv,
            tm):
    bsz, seq, d = x.shape
    half = HEAD_DIM // 2
    inv_freq = jnp.power(jnp.float32(ROPE_THETA), -jnp.arange(half, dtype=F32) / half)
    invf = jnp.tile(inv_freq, LANES // half).reshape(1, LANES)
    qn = jnp.tile(q_norm, SWA_Q_HEADS).reshape(1, SWA_WIDTH)
    kn = jnp.tile(k_norm, SWA_Q_HEADS).reshape(1, SWA_WIDTH)
    h512 = np.arange(SWA_WIDTH) // HEAD_DIM
    bd512 = jnp.asarray((h512[:, None] == h512[None, :]).astype(np.float32), BF16)
    kv_lane = (h512 // SWA_GROUP) * HEAD_DIM + np.arange(SWA_WIDTH) % HEAD_DIM
    rep = jnp.asarray((np.arange(SWA_KV_HEADS * HEAD_DIM)[:, None] == kv_lane[None, :]
                       ).astype(np.float32), BF16)
    row = lambda b, i: (b, i, 0)
    vec = lambda b, i: (b, 0, 0)
    const = lambda b, i: (0, 0)
    widths = (GDN_WIDTH, GDN_WIDTH, GDN_WIDTH, GDN_WIDTH, LANES, SWA_WIDTH, SWA_WIDTH, SWA_WIDTH)
    dtypes = (BF16, BF16, BF16, BF16, F32, BF16, BF16, BF16)
    return pl.pallas_call(
        _inproj_kernel,
        grid=(bsz, seq // tm),
        in_specs=[pl.BlockSpec((None, tm, d), row),
                  pl.BlockSpec((None, 1, d), vec),
                  pl.BlockSpec((None, 1, d), vec),
                  pl.BlockSpec((1, d), const),
                  pl.BlockSpec((None, tm, 1), row)]
                 + [pl.BlockSpec(w.shape, const)
                    for w in (invf, qn, kn, bd512, rep, conv_w, wqkv, wz, wab, wsq, wsk, wsv)],
        out_specs=[pl.BlockSpec((None, tm, w), row) for w in widths],
        out_shape=[jax.ShapeDtypeStruct((bsz, seq, w), t) for w, t in zip(widths, dtypes)],
        scratch_shapes=[pltpu.VMEM((tm + 8, 3 * GDN_WIDTH), F32)],
        compiler_params=pltpu.CompilerParams(
            dimension_semantics=("parallel", "arbitrary"), vmem_limit_bytes=VMEM_LIMIT),
        name="inproj",
    )(x, shift, scale, norm_w, positions.reshape(bsz, seq, 1), invf, qn, kn, bd512, rep, conv_w,
      wqkv, wz, wab, wsq, wsk, wsv)


def _gdn_consts(tb):
    r = np.arange(CHUNK)[:, None]
    cidx = np.arange(TILE)[None, :] % HEAD_DIM
    low = (r >= cidx).astype(np.float32)
    strict = (r > cidx).astype(np.float32)
    eye = (r == cidx).astype(np.float32)
    upper = (r <= cidx).astype(np.float32)
    hb = np.arange(TILE) // HEAD_DIM
    blk = (hb[:, None] == hb[None, :]).astype(np.float32)
    h512 = np.arange(GDN_WIDTH) // HEAD_DIM
    bd512 = (h512[:, None] == h512[None, :]).astype(np.float32)
    e128 = np.zeros((LANES, 2 * GDN_WIDTH), np.float32)
    for j in range(2 * GDN_HEADS):
        e128[j, j * HEAD_DIM:(j + 1) * HEAD_DIM] = 1.0
    t = np.arange(tb)
    lbd = ((t[:, None] >= t[None, :]) & (t[:, None] // CHUNK == t[None, :] // CHUNK))
    return dict(low=jnp.asarray(low), strict=jnp.asarray(strict), eye=jnp.asarray(eye),
                upper=jnp.asarray(upper), blk=jnp.asarray(blk, BF16),
                bd512=jnp.asarray(bd512, BF16), e128=jnp.asarray(e128, BF16),
                lbd=jnp.asarray(lbd.astype(np.float32), BF16))


def _gdn_kernel(q_ref, k_ref, v_ref, z_ref, ab_ref, alog_ref, dt_ref, gamma_ref,
                bd512_ref, e128_ref, lbd_ref, low_ref, strict_ref, eye_ref, upper_ref, blk_ref,
                out_ref,
                s_ref, g_s, b_s, gc_s, o_s, u_s, at_s, kd_s, wq_s, eg_s,
                *, tb, nb):
    step = pl.program_id(0)

    @pl.when(step == 0)
    def _():
        s_ref[...] = jnp.zeros_like(s_ref)

    bd512 = bd512_ref[...]
    low = low_ref[...] > 0.5
    strict = strict_ref[...] > 0.5
    upper = upper_ref[...] > 0.5
    eye = eye_ref[...]
    blk = blk_ref[...]
    blk_f = blk.astype(F32)
    n_tiles = GDN_WIDTH // TILE
    lane_tiles = [slice(g * TILE, (g + 1) * TILE) for g in range(n_tiles)]
    n_chunks = tb // CHUNK

    def prepare(b):
        ab = ab_ref[b]
        lane = lax.broadcasted_iota(jnp.int32, ab.shape, 1)
        gb = jnp.where(lane < GDN_HEADS,
                       -jnp.exp(alog_ref[...]) * _softplus(ab + dt_ref[...]),
                       _sigmoid(ab))
        lbd = lbd_ref[...]
        gcum = None
        for p in _split(gb, 3):
            t = _dot(lbd, p)
            gcum = t if gcum is None else gcum + t
        yield
        e128 = e128_ref[...]
        gbe = _dot_split_lhs(gb, e128, 3)
        g_s[b] = gbe[:, :GDN_WIDTH]
        b_s[b] = gbe[:, GDN_WIDTH:]
        yield
        gc_s[b] = _dot_split_lhs(gcum, e128[:, :GDN_WIDTH], 3)

    def bdiag(t):
        tb16 = t.astype(BF16)
        return jnp.concatenate([tb16] * HEADS_PER_TILE, axis=0) * blk

    def intra_chunk(units, filler):
        n = range(len(units))
        bi = [b for b, _, _ in units]
        rows = [pl.ds(c * CHUNK, CHUNK) for _, c, _ in units]
        sl = [lane_tiles[g] for _, _, g in units]
        kc = [k_ref[bi[i], rows[i], sl[i]].astype(F32) for i in n]
        qc = [q_ref[bi[i], rows[i], sl[i]].astype(F32) for i in n]
        bc = [b_s[bi[i], rows[i], sl[i]] for i in n]
        gcc = [gc_s[bi[i], rows[i], sl[i]] for i in n]
        grow = [jnp.sum(jnp.where(upper, g_s[bi[i], rows[i], sl[i]], 0.0), axis=0, keepdims=True)
                for i in n]
        decay = [jnp.where(low, jnp.exp(jnp.where(low, gcc[i] - grow[i], 0.0)), 0.0) for i in n]
        kb = [kc[i] * bc[i] for i in n]
        mm = [_dot_nt(jnp.concatenate([kb[i], qc[i]], axis=0).astype(BF16), bdiag(kc[i]))
              for i in n]
        next(filler, None)
        m = [jnp.where(strict, mm[i][:CHUNK] * decay[i], 0.0) for i in n]
        for i in n:
            at_s[bi[i], rows[i], sl[i]] = (mm[i][CHUNK:] * decay[i]).astype(BF16)

        pw = [_dot(m[i].astype(BF16), bdiag(m[i])) for i in n]
        tinv = [eye - m[i] for i in n]
        for _ in range(4):
            both = [_dot(jnp.concatenate([pw[i], tinv[i]], axis=0).astype(BF16), bdiag(pw[i]))
                    for i in n]
            next(filler, None)
            pw = [both[i][:CHUNK] for i in n]
            tinv = [tinv[i] + both[i][CHUNK:] for i in n]
        last = [_dot(tinv[i].astype(BF16), bdiag(pw[i])) for i in n]
        next(filler, None)
        tinv16 = [(tinv[i] + last[i]).astype(BF16) for i in n]

        eg = [jnp.exp(gcc[i]) for i in n]
        u = [_dot(tinv16[i], bdiag(v_ref[bi[i], rows[i], sl[i]].astype(F32) * bc[i])) for i in n]
        w = [_dot(tinv16[i], bdiag(kb[i] * eg[i])) for i in n]
        for i, (b, c, g) in enumerate(units):
            glast = gcc[i][CHUNK - 1:CHUNK, :]
            u_s[b, rows[i], sl[i]] = u[i]
            wq_s[b, c, :CHUNK, sl[i]] = w[i].astype(BF16)
            wq_s[b, c, CHUNK:, sl[i]] = (qc[i] * eg[i]).astype(BF16)
            kd_s[b, rows[i], sl[i]] = (kc[i] * jnp.exp(glast - gcc[i])).astype(BF16)
            eg_s[b, pl.ds(c * 8, 1), sl[i]] = jnp.exp(glast)

    def inter_chunk(c):
        chains = [(b, g) for b in range(nb) for g in range(n_tiles)]
        n = range(len(chains))
        rows = pl.ds(c * CHUNK, CHUNK)
        sl = [lane_tiles[g] for _, g in chains]
        bi = [b for b, _ in chains]
        state = [s_ref[b, g] for b, g in chains]
        ws_qs = [_dot(wq_s[bi[i], c, :, sl[i]], state[i].astype(BF16)) for i in n]
        vnew = [u_s[bi[i], rows, sl[i]] - ws_qs[i][:CHUNK] for i in n]
        intra = [_dot(at_s[bi[i], rows, sl[i]], bdiag(vnew[i])) for i in n]
        kv = [_dot_tn(kd_s[bi[i], rows, sl[i]], vnew[i].astype(BF16)) for i in n]
        for i, (b, g) in enumerate(chains):
            o_s[b, rows, sl[i]] = ws_qs[i][CHUNK:] + intra[i]
            s_ref[b, g] = state[i] * eg_s[b, pl.ds(c * 8, 1), sl[i]] + kv[i] * blk_f

    def finish(b):
        o = o_s[b]
        ms = _head_sums(o * o, bd512) * (1.0 / HEAD_DIM)
        zf = z_ref[b].astype(F32)
        out_ref[b] = (o * lax.rsqrt(ms + EPS) * gamma_ref[...] * _silu(zf)).astype(BF16)

    def drain(gen):
        for _ in gen:
            pass

    drain(prepare(0))
    for b in range(nb):
        filler = prepare(b + 1) if b + 1 < nb else iter(())
        for c0 in range(0, n_chunks, INTRA_CHUNKS):
            intra_chunk([(b, c, g) for c in range(c0, min(c0 + INTRA_CHUNKS, n_chunks))
                         for g in range(n_tiles)], filler)
        drain(filler)
    for c in range(n_chunks):
        inter_chunk(c)
    for b in range(nb):
        finish(b)


def _gdn(q, k, v, z, ab, a_log, dt_bias, out_norm, tb):
    nb, seq, _ = q.shape
    cs = _gdn_consts(tb)
    pad = lambda v: jnp.zeros((1, LANES), F32).at[0, :GDN_HEADS].set(v)
    gamma = jnp.tile(out_norm, GDN_HEADS).reshape(1, GDN_WIDTH)
    row = lambda i: (0, i, 0)
    const = lambda i: (0, 0)
    consts = (cs["bd512"], cs["e128"], cs["lbd"], cs["low"], cs["strict"], cs["eye"],
              cs["upper"], cs["blk"])
    small = (pad(a_log), pad(dt_bias), gamma)
    wide = pltpu.VMEM((nb, tb, GDN_WIDTH), F32)
    wide16 = pltpu.VMEM((nb, tb, GDN_WIDTH), BF16)
    n_chunks = tb // CHUNK
    return pl.pallas_call(
        functools.partial(_gdn_kernel, tb=tb, nb=nb),
        grid=(seq // tb,),
        in_specs=[pl.BlockSpec((nb, tb, GDN_WIDTH), row)] * 4
                 + [pl.BlockSpec((nb, tb, LANES), row)]
                 + [pl.BlockSpec(a.shape, const) for a in small + consts],
        out_specs=pl.BlockSpec((nb, tb, GDN_WIDTH), row),
        out_shape=jax.ShapeDtypeStruct((nb, seq, GDN_WIDTH), BF16),
        scratch_shapes=[pltpu.VMEM((nb, GDN_WIDTH // TILE, TILE, TILE), F32),
                        wide, wide, wide, wide, wide,
                        wide16, wide16,
                        pltpu.VMEM((nb, n_chunks, 2 * CHUNK, GDN_WIDTH), BF16),
                        pltpu.VMEM((nb, n_chunks * 8, GDN_WIDTH), F32)],
        compiler_params=pltpu.CompilerParams(
            dimension_semantics=("arbitrary",), vmem_limit_bytes=VMEM_LIMIT),
        name="gdn",
    )(q, k, v, z, ab, *small, *consts)


def _swa_kernel(q_ref, k0_ref, k1_ref, k2_ref, v0_ref, v1_ref, v2_ref, sink_ref, valid_ref,
                hmask_ref, ones_ref, out_ref):
    step = pl.program_id(1)
    hmask = hmask_ref[...]
    heads = range(SWA_GROUP)
    k_blocks = (k0_ref, k1_ref, k2_ref)
    v_blocks = (v0_ref, v1_ref, v2_ref)
    in_window = valid_ref[...] > 0.5
    valid = (in_window & ((step > 0) | (valid_ref[...] > 1.5)), in_window)
    units = [(j, g) for j in range(SWA_QBLOCKS) for g in range(SWA_KV_HEADS)]
    n = range(len(units))
    rows = [slice(j * BLOCK, (j + 1) * BLOCK) for j, _ in units]
    sl = [slice(g * TILE, (g + 1) * TILE) for _, g in units]
    kband = [jnp.concatenate([k_blocks[j][:, sl[i]], k_blocks[j + 1][:, sl[i]]], axis=0)
             for i, (j, _) in enumerate(units)]
    vband = [jnp.concatenate([v_blocks[j][:, sl[i]], v_blocks[j + 1][:, sl[i]]], axis=0)
             for i, (j, _) in enumerate(units)]
    lhs = [jnp.concatenate([q_ref[rows[i], sl[i]].astype(F32) * hmask[h:h + 1] for h in heads],
                           axis=0).astype(BF16) for i in n]
    s = [jnp.where(valid[units[i][0]], _dot_nt(lhs[i], kband[i]), -jnp.inf) for i in n]
    sink = [sink_ref[g] for _, g in units]
    mx = [jnp.maximum(jnp.max(s[i], axis=1, keepdims=True), sink[i]) for i in n]
    p = [jnp.exp(s[i] - mx[i]).astype(BF16) for i in n]
    psum = [_dot(p[i], ones_ref[...]) for i in n]
    pv = [_dot(p[i], vband[i]) for i in n]
    for i in n:
        rinv = 1.0 / (psum[i] + jnp.exp(sink[i] - mx[i]))
        out = None
        for h in heads:
            hrows = slice(h * BLOCK, (h + 1) * BLOCK)
            scale = jnp.concatenate([rinv[hrows]] * (TILE // LANES), axis=1) * hmask[h:h + 1]
            term = pv[i][hrows] * scale
            out = term if out is None else out + term
        out_ref[rows[i], sl[i]] = out.astype(BF16)


def _swa(sq, sk, sv, sinks):
    bsz, seq, _ = sq.shape
    sink_rows = jnp.repeat(sinks.astype(F32), BLOCK).reshape(SWA_KV_HEADS, SWA_GROUP * BLOCK, 1)
    qi = np.arange(BLOCK)[:, None]
    kj = np.arange(2 * BLOCK)[None, :]
    rel = qi + BLOCK - kj
    inwin = ((rel >= 0) & (rel < WINDOW)).astype(np.float32)
    valid = inwin * np.where(kj >= BLOCK, 2.0, 1.0)
    valid = jnp.asarray(np.tile(valid, (SWA_GROUP, 1)), F32)
    hm = (np.arange(TILE)[None, :] // HEAD_DIM == np.arange(SWA_GROUP)[:, None]).astype(np.float32)
    const2 = lambda b, i: (0, 0)
    const3 = lambda b, i: (0, 0, 0)
    blk = (None, BLOCK, SWA_WIDTH)
    key_blocks = [pl.BlockSpec(blk, lambda b, i: (b, jnp.maximum(SWA_QBLOCKS * i - 1, 0), 0)),
                  pl.BlockSpec(blk, lambda b, i: (b, SWA_QBLOCKS * i, 0)),
                  pl.BlockSpec(blk, lambda b, i: (b, SWA_QBLOCKS * i + 1, 0))]
    qrows = (None, SWA_QBLOCKS * BLOCK, SWA_WIDTH)
    return pl.pallas_call(
        _swa_kernel,
        grid=(bsz, seq // (SWA_QBLOCKS * BLOCK)),
        in_specs=[pl.BlockSpec(qrows, lambda b, i: (b, i, 0))] + key_blocks + key_blocks
                 + [pl.BlockSpec(sink_rows.shape, const3),
                    pl.BlockSpec(valid.shape, const2),
                    pl.BlockSpec(hm.shape, const2),
                    pl.BlockSpec((2 * BLOCK, LANES), const2)],
        out_specs=pl.BlockSpec(qrows, lambda b, i: (b, i, 0)),
        out_shape=jax.ShapeDtypeStruct((bsz, seq, SWA_WIDTH), BF16),
        compiler_params=pltpu.CompilerParams(
            dimension_semantics=("parallel", "parallel"), vmem_limit_bytes=VMEM_LIMIT),
        name="swa",
    )(sq, sk, sk, sk, sv, sv, sv, sink_rows, valid, jnp.asarray(hm),
      jnp.ones((2 * BLOCK, LANES), BF16))


def _outproj_kernel(gdn_ref, swa_ref, x_ref, gate_ref, sh_ref, sc_ref, nw_ref, wo1_ref, wo2_ref,
                    wgrp_ref, bgrp_ref, wrt_ref, brt_ref, x1_ref, h2_ref, comb_ref):
    mixed = _dot(gdn_ref[...], wo1_ref[...]) + _dot(swa_ref[...], wo2_ref[...])
    x1 = x_ref[...] + gate_ref[...] * mixed
    x1_ref[...] = x1
    ms = jnp.mean(x1 * x1, axis=-1, keepdims=True)
    h2 = x1 * lax.rsqrt(ms + EPS) * nw_ref[...]
    h2 = h2 * (1.0 + sc_ref[...]) + sh_ref[...]
    h2b = h2.astype(BF16)
    h2_ref[...] = h2

    lg = _dot(h2b, wgrp_ref[...]) + bgrp_ref[...]
    lane = lax.broadcasted_iota(jnp.int32, lg.shape, 1).astype(F32)
    none = jnp.float32(LANES)
    lg = jnp.where(lane < N_GROUPS, lg, -jnp.inf)
    gmax = jnp.max(lg, axis=1, keepdims=True)
    gidx = jnp.min(jnp.where(lg == gmax, lane, none), axis=1, keepdims=True)
    p_group = 1.0 / jnp.sum(jnp.exp(lg - gmax), axis=1, keepdims=True)
    le = _dot(h2b, wrt_ref[...]) + brt_ref[...]
    lo = gidx * EXPERTS_PER_GROUP
    in_group = (lane >= lo) & (lane < lo + EXPERTS_PER_GROUP)
    le = jnp.where(in_group, le, -jnp.inf)
    pe = jnp.exp(le - jnp.max(le, axis=1, keepdims=True))
    pe = pe / jnp.sum(pe, axis=1, keepdims=True)
    p1 = jnp.max(pe, axis=1, keepdims=True)
    i1 = jnp.min(jnp.where(in_group & (pe == p1), lane, none), axis=1, keepdims=True)
    rest = in_group & (lane != i1)
    pe2 = jnp.where(rest, pe, -1.0)
    p2 = jnp.max(pe2, axis=1, keepdims=True)
    i2 = jnp.min(jnp.where(rest & (pe2 == p2), lane, none), axis=1, keepdims=True)
    wsum = p1 + p2
    comb_ref[...] = (jnp.where(lane == i1, 1.0, 0.0)
                     + jnp.where(lane == i2 + N_EXPERTS, 1.0, 0.0)
                     + jnp.where(lane == W1_LANE, p1 / wsum * p_group, 0.0)
                     + jnp.where(lane == W1_LANE + 1, p2 / wsum * p_group, 0.0))


def _outproj(gdn, swa, x, gate, shift, scale, norm_w, wo1, wo2, wgrp, bgrp, wrt, brt, tm):
    bsz, seq, d = x.shape
    row = lambda b, i: (b, i, 0)
    vec = lambda b, i: (b, 0, 0)
    const = lambda b, i: (0, 0)
    return pl.pallas_call(
        _outproj_kernel,
        grid=(bsz, seq // tm),
        in_specs=[pl.BlockSpec((None, tm, GDN_WIDTH), row),
                  pl.BlockSpec((None, tm, SWA_WIDTH), row),
                  pl.BlockSpec((None, tm, d), row),
                  pl.BlockSpec((None, 1, d), vec),
                  pl.BlockSpec((None, 1, d), vec),
                  pl.BlockSpec((None, 1, d), vec),
                  pl.BlockSpec((1, d), const)]
                 + [pl.BlockSpec(a.shape, const) for a in (wo1, wo2, wgrp, bgrp, wrt, brt)],
        out_specs=[pl.BlockSpec((None, tm, d), row),
                   pl.BlockSpec((None, tm, d), row),
                   pl.BlockSpec((None, tm, LANES), row)],
        out_shape=[jax.ShapeDtypeStruct((bsz, seq, d), F32),
                   jax.ShapeDtypeStruct((bsz, seq, d), F32),
                   jax.ShapeDtypeStruct((bsz, seq, LANES), F32)],
        compiler_params=pltpu.CompilerParams(
            dimension_semantics=("parallel", "parallel"), vmem_limit_bytes=VMEM_LIMIT),
        name="outproj",
    )(gdn, swa, x, gate, shift, scale, norm_w, wo1, wo2, wgrp, bgrp, wrt, brt)


def _route_kernel(info_ref, lstrict_ref, ustrict_ref, sel_ref, pos_ref, cnt_ref,
                  total_ref, run_ref, off_ref, *, tile):
    phase = pl.program_id(0)
    step = pl.program_id(1)
    info = info_ref[...]
    lane = lax.broadcasted_iota(jnp.int32, info.shape, 1)
    first = jnp.where(lane < N_EXPERTS, info, 0.0)
    second = pltpu.roll(jnp.where((lane >= N_EXPERTS) & (lane < 2 * N_EXPERTS), info, 0.0),
                        LANES - N_EXPERTS, 1)
    ind = first + second
    colsum = jnp.sum(ind, axis=0, keepdims=True)

    @pl.when((phase == 0) & (step == 0))
    def _():
        total_ref[...] = jnp.zeros_like(total_ref)

    @pl.when(phase == 0)
    def _():
        total_ref[...] += colsum

    @pl.when((phase == 1) & (step == 0))
    def _():
        cap = jnp.ceil(total_ref[...] * (1.0 / tile)) * tile
        cap8 = jnp.broadcast_to(cap, (8, LANES))
        off_ref[...] = _dot_split_lhs(cap8, ustrict_ref[...], 3)[0:1]
        run_ref[...] = jnp.zeros_like(run_ref)

    @pl.when(phase == 1)
    def _():
        base = _dot(lstrict_ref[...], ind.astype(BF16)) + run_ref[...] + off_ref[...]
        pos1 = jnp.sum(first * base, axis=1, keepdims=True)
        pos2 = jnp.sum(second * base, axis=1, keepdims=True)
        run_ref[...] += colsum
        both = jnp.where(lane == 0, pos1, 0.0) + jnp.where(lane == 1, pos2, 0.0)
        rows = None
        for p in _split(both, 3):
            t = _dot_nt(sel_ref[...], p)
            rows = t if rows is None else rows + t
        pos_ref[...] = rows.astype(jnp.int32)

    cnt_ref[...] = jnp.broadcast_to(total_ref[...], cnt_ref.shape)


def _route(info, tt, tile):
    n_tok = info.shape[0]
    r = np.arange(tt)
    lstrict = jnp.asarray((r[:, None] > r[None, :]).astype(np.float32), BF16)
    l = np.arange(LANES)
    ustrict = jnp.asarray((l[:, None] < l[None, :]).astype(np.float32), BF16)
    sel = jnp.asarray(np.eye(8, LANES, dtype=np.float32), BF16)
    const = lambda p, i: (0, 0)
    return pl.pallas_call(
        functools.partial(_route_kernel, tile=tile),
        grid=(2, n_tok // tt),
        in_specs=[pl.BlockSpec((tt, LANES), lambda p, i: (i, 0)),
                  pl.BlockSpec((tt, tt), const),
                  pl.BlockSpec((LANES, LANES), const),
                  pl.BlockSpec((8, LANES), const)],
        out_specs=[pl.BlockSpec((8, tt), lambda p, i: (0, i * p)),
                   pl.BlockSpec((8, LANES), const)],
        out_shape=[jax.ShapeDtypeStruct((8, n_tok), jnp.int32),
                   jax.ShapeDtypeStruct((8, LANES), F32)],
        scratch_shapes=[pltpu.VMEM((1, LANES), F32)] * 3,
        compiler_params=pltpu.CompilerParams(
            dimension_semantics=("arbitrary", "arbitrary"), vmem_limit_bytes=VMEM_LIMIT),
        name="route",
    )(info, lstrict, ustrict, sel)


def _row_copy(src, src_row, dst, dst_row, sem):
    return pltpu.make_async_copy(src.at[pl.ds(src_row, 1)], dst.at[pl.ds(dst_row, 1)], sem)


def _for_each_row(lo, hi, fn):
    def trip(t, carry):
        for j in range(ROW_UNROLL):
            fn(t * ROW_UNROLL + j)
        return carry

    lax.fori_loop(lo // ROW_UNROLL, hi // ROW_UNROLL, trip, 0)


def _dispatch_kernel(cnt_ref, off_ref, cap_ref, nu_ref, pos1_ref, pos2_ref, h2_ref, xs_ref,
                     zero_ref, sem):
    step = pl.program_id(0)
    tm = h2_ref.shape[0]
    tile = zero_ref.shape[0]

    @pl.when(step == 0)
    def _():
        zero_ref[...] = jnp.zeros_like(zero_ref)

        def tail_copy(t):
            return pltpu.make_async_copy(zero_ref, xs_ref.at[pl.ds(t * tile, tile)], sem)

        def tail_start(t, c):
            tail_copy(t).start()
            return c

        def tail_wait(t, c):
            tail_copy(t).wait()
            return c

        lax.fori_loop(nu_ref[0], xs_ref.shape[0] // tile, tail_start, 0)
        lax.fori_loop(nu_ref[0], xs_ref.shape[0] // tile, tail_wait, 0)

        def per_expert(e, carry):
            lo = off_ref[e] + cnt_ref[e]
            hi = off_ref[e] + cap_ref[e]

            def start(s, c):
                _row_copy(zero_ref, 0, xs_ref, s, sem).start()
                return c

            def wait(s, c):
                _row_copy(zero_ref, 0, xs_ref, s, sem).wait()
                return c

            lax.fori_loop(lo, hi, start, 0)
            lax.fori_loop(lo, hi, wait, 0)
            return carry

        lax.fori_loop(0, N_EXPERTS, per_expert, 0)

    def copies(r):
        return (_row_copy(h2_ref, r, xs_ref, pos1_ref[r], sem),
                _row_copy(h2_ref, r, xs_ref, pos2_ref[r], sem))

    _for_each_row(0, tm, lambda r: [cp.start(priority=p) for p, cp in enumerate(copies(r))])
    _for_each_row(0, tm, lambda r: [cp.wait() for cp in copies(r)])


def _dispatch(h2, pos1, pos2, cnt, off, cap, n_used, n_rows, tile, tm):
    n_tok, d = h2.shape
    smem_tok = pl.BlockSpec((tm,), lambda i, *_: (i,), memory_space=pltpu.SMEM)
    return pl.pallas_call(
        _dispatch_kernel,
        grid_spec=pltpu.PrefetchScalarGridSpec(
            num_scalar_prefetch=4,
            grid=(n_tok // tm,),
            in_specs=[smem_tok, smem_tok, pl.BlockSpec((tm, d), lambda i, *_: (i, 0))],
            out_specs=pl.BlockSpec(memory_space=pl.ANY),
            scratch_shapes=[pltpu.VMEM((tile, d), F32), pltpu.SemaphoreType.DMA]),
        out_shape=jax.ShapeDtypeStruct((n_rows, d), F32),
        compiler_params=pltpu.CompilerParams(
            dimension_semantics=("arbitrary",), vmem_limit_bytes=VMEM_LIMIT),
        name="dispatch",
    )(cnt, off, cap, n_used, pos1, pos2, h2)


def _gmm_kernel(te_ref, nu_ref, xs_ref, wg_ref, wu_ref, wd_ref, ys_ref):
    step = pl.program_id(0)

    @pl.when(step < nu_ref[0])
    def _():
        x = xs_ref[...].astype(BF16)
        hid = _silu(_dot(x, wg_ref[...].astype(BF16))) * _dot(x, wu_ref[...].astype(BF16))
        ys_ref[...] = _dot(hid.astype(BF16), wd_ref[...].astype(BF16))

    @pl.when(step >= nu_ref[0])
    def _():
        ys_ref[...] = jnp.zeros_like(ys_ref)


def _gmm(xs, tile_expert, n_used, wg, wu, wd, tile):
    n_rows, d = xs.shape
    de = wg.shape[-1]
    wsel = lambda i, te, nu: (te[i], 0, 0)
    return pl.pallas_call(
        _gmm_kernel,
        grid_spec=pltpu.PrefetchScalarGridSpec(
            num_scalar_prefetch=2,
            grid=(n_rows // tile,),
            in_specs=[pl.BlockSpec((tile, d), lambda i, te, nu: (jnp.where(i < nu[0], i, 0), 0)),
                      pl.BlockSpec((None, d, de), wsel),
                      pl.BlockSpec((None, d, de), wsel),
                      pl.BlockSpec((None, de, d), wsel)],
            out_specs=pl.BlockSpec((tile, d), lambda i, te, nu: (i, 0))),
        out_shape=jax.ShapeDtypeStruct((n_rows, d), F32),
        compiler_params=pltpu.CompilerParams(
            dimension_semantics=("arbitrary",), vmem_limit_bytes=VMEM_LIMIT),
        name="gmm",
    )(tile_expert, n_used, xs, wg, wu, wd)


def _combine_kernel(pos1_ref, pos2_ref, x1_ref, info_ref, gate_ref, ys_ref, out_ref, ybuf, sem):
    tm = x1_ref.shape[0]

    half = tm // 2

    def copies(r, part):
        return (_row_copy(ys_ref, pos1_ref[r], ybuf.at[0], r, sem.at[part]),
                _row_copy(ys_ref, pos2_ref[r], ybuf.at[1], r, sem.at[part]))

    def start(r, part):
        for p, cp in enumerate(copies(r, part)):
            cp.start(priority=p)

    for part, lo in enumerate((0, half)):
        _for_each_row(lo, lo + half, functools.partial(start, part=part))
    for part, lo in enumerate((0, half)):
        rows = pl.ds(lo, half)
        _for_each_row(lo, lo + half, lambda r: [cp.wait() for cp in copies(r, part)])
        info = info_ref[rows, :]
        lane = lax.broadcasted_iota(jnp.int32, info.shape, 1)
        w1 = jnp.sum(jnp.where(lane == W1_LANE, info, 0.0), axis=1, keepdims=True)
        w2 = jnp.sum(jnp.where(lane == W1_LANE + 1, info, 0.0), axis=1, keepdims=True)
        out_ref[rows, :] = x1_ref[rows, :] + gate_ref[...] * (w1 * ybuf[0, rows, :]
                                                              + w2 * ybuf[1, rows, :])


def _combine(ys, pos1, pos2, x1, info, gate, seq, tm):
    n_tok, d = x1.shape
    smem_tok = pl.BlockSpec((tm,), lambda i: (i,), memory_space=pltpu.SMEM)
    return pl.pallas_call(
        _combine_kernel,
        grid=(n_tok // tm,),
        in_specs=[smem_tok, smem_tok,
                  pl.BlockSpec((tm, d), lambda i: (i, 0)),
                  pl.BlockSpec((tm, LANES), lambda i: (i, 0)),
                  pl.BlockSpec((None, 1, d), lambda i: (i * tm // seq, 0, 0)),
                  pl.BlockSpec(memory_space=pl.ANY)],
        out_specs=pl.BlockSpec((tm, d), lambda i: (i, 0)),
        out_shape=jax.ShapeDtypeStruct((n_tok, d), F32),
        scratch_shapes=[pltpu.VMEM((2, tm, d), F32), pltpu.SemaphoreType.DMA((2,))],
        compiler_params=pltpu.CompilerParams(
            dimension_semantics=("arbitrary",), vmem_limit_bytes=VMEM_LIMIT),
        name="combine",
    )(pos1, pos2, x1, info, gate, ys)


def _moe(h2, info, x1, gate, wg, wu, wd):
    bsz, seq, d = x1.shape
    n_tok = bsz * seq
    tile = MOE_TILE
    n_rows = 2 * n_tok + N_EXPERTS * tile
    info = info.reshape(n_tok, LANES)
    pos, cnt = _route(info, min(1024, n_tok), tile)
    cnt = cnt[0, :N_EXPERTS].astype(jnp.int32)
    cap = (cnt + tile - 1) // tile * tile
    end = jnp.cumsum(cap)
    off = end - cap
    tile_start = jnp.arange(n_rows // tile, dtype=jnp.int32) * tile
    tile_expert = jnp.minimum(
        jnp.sum((end[None, :] <= tile_start[:, None]).astype(jnp.int32), axis=1), N_EXPERTS - 1)
    n_used = (end[-1:] // tile).astype(jnp.int32)
    tm = min(1024, n_tok)
    xs = _dispatch(h2.reshape(n_tok, d), pos[0], pos[1], cnt, off, cap, n_used, n_rows, tile, tm)
    ys = _gmm(xs, tile_expert, n_used, wg, wu, wd, tile)
    out = _combine(ys, pos[0], pos[1], x1.reshape(n_tok, d), info, gate, seq, tm)
    return out.reshape(bsz, seq, d)


def _layer(x, c_mod, positions, norm_mix, w_in, conv_w, a_log, dt_bias, gdn_out_norm, q_norm,
           k_norm, sinks, w_out, norm_ffn, w_group, b_group, w_router, b_router, w_gate, w_up,
           w_down):
    bsz, seq, d = x.shape
    shift1, scale1, gate1, shift2, scale2, gate2 = [
        m.reshape(bsz, 1, d) for m in jnp.split(c_mod, 6, axis=-1)]

    o = 0
    w_qkv = w_in[:, o:o + 3 * GDN_WIDTH]; o += 3 * GDN_WIDTH
    w_z = w_in[:, o:o + GDN_WIDTH]; o += GDN_WIDTH
    w_ab = w_in[:, o:o + 2 * GDN_HEADS]; o += 2 * GDN_HEADS
    w_sq = w_in[:, o:o + SWA_WIDTH]; o += SWA_WIDTH
    kvw = SWA_KV_HEADS * HEAD_DIM
    w_sk = w_in[:, o:o + kvw]; o += kvw
    w_sv = w_in[:, o:o + kvw]
    w_ab = jnp.zeros((d, LANES), F32).at[:, :2 * GDN_HEADS].set(w_ab)
    tm = min(512, seq)
    gq, gk, gv, z, ab, sq, sk, sv = _inproj(
        x, shift1, scale1, norm_mix.reshape(1, d), positions, q_norm, k_norm, conv_w,
        w_qkv.astype(BF16), w_z.astype(BF16),
        w_ab.astype(BF16), w_sq.astype(BF16), w_sk.astype(BF16), w_sv.astype(BF16), tm)

    gdn = _gdn(gq, gk, gv, z, ab, a_log, dt_bias, gdn_out_norm, min(256, seq))
    swa = _swa(sq, sk, sv, sinks)

    padw = lambda w: jnp.zeros((d, LANES), F32).at[:, :w.shape[1]].set(w).astype(BF16)
    padb = lambda b: jnp.zeros((1, LANES), F32).at[0, :b.shape[0]].set(b)
    x1, h2, comb = _outproj(
        gdn, swa, x, gate1, shift2, scale2, norm_ffn.reshape(1, d),
        w_out[:GDN_WIDTH].astype(BF16), w_out[GDN_WIDTH:].astype(BF16),
        padw(w_group), padb(b_group), padw(w_router), padb(b_router), tm)

    return _moe(h2, comb, x1, gate2, w_gate, w_up, w_down)


def kernel(x, c, positions, w_ada, b_ada, norm_mix, w_in, conv_w, a_log, dt_bias, gdn_out_norm,
           q_norm, k_norm, sinks, w_out, norm_ffn, w_group, b_group, w_router, b_router, w_gate,
           w_up, w_down):
    depth = w_ada.shape[0]
    for l in range(depth):
        c_mod = _ada(c, w_ada[l], b_ada[l])
        x = _layer(x, c_mod, positions, norm_mix[l], w_in[l], conv_w[l], a_log[l], dt_bias[l],
                   gdn_out_norm[l], q_norm[l], k_norm[l], sinks[l], w_out[l], norm_ffn[l],
                   w_group[l], b_group[l], w_router[l], b_router[l], w_gate[l], w_up[l],
                   w_down[l])
    return x
```

```python
import functools

import numpy as np
import jax
import jax.numpy as jnp
from jax import lax
from jax.experimental import pallas as pl
from jax.experimental.pallas import tpu as pltpu

F32 = jnp.float32
BF16 = jnp.bfloat16

HEAD_DIM = 64
GDN_HEADS = 8
GDN_WIDTH = GDN_HEADS * HEAD_DIM
CONV_K = 4
CHUNK = 64
SWA_Q_HEADS = 8
SWA_KV_HEADS = 2
SWA_GROUP = SWA_Q_HEADS // SWA_KV_HEADS
SWA_WIDTH = SWA_Q_HEADS * HEAD_DIM
WINDOW = 128
BLOCK = 128
ROPE_THETA = 10000.0
N_GROUPS = 4
EXPERTS_PER_GROUP = 8
N_EXPERTS = N_GROUPS * EXPERTS_PER_GROUP
EPS = 1e-6
LANES = 128
W1_LANE = 2 * N_EXPERTS
MOE_TILE = 256
ROW_UNROLL = 8
HEADS_PER_TILE = 4
SWA_QBLOCKS = 2
INTRA_CHUNKS = 4
TILE = HEADS_PER_TILE * HEAD_DIM
VMEM_LIMIT = 56 * 1024 * 1024


def _dot(a, b):
    return jnp.dot(a, b, preferred_element_type=F32)


def _dot_nt(a, b):
    return lax.dot_general(a, b, (((1,), (1,)), ((), ())), preferred_element_type=F32)


def _dot_tn(a, b):
    return lax.dot_general(a, b, (((0,), (0,)), ((), ())), preferred_element_type=F32)


def _split(x, n):
    parts = []
    r = x
    for i in range(n):
        p = r.astype(BF16)
        parts.append(p)
        if i + 1 < n:
            r = r - p.astype(F32)
    return parts


def _dot_split_lhs(x, w, n):
    acc = None
    for p in _split(x, n):
        t = _dot(p, w)
        acc = t if acc is None else acc + t
    return acc


def _head_sums(sq, ones_blocks):
    return _dot(sq.astype(BF16), ones_blocks)


def _sigmoid(x):
    return 0.5 + 0.5 * jnp.tanh(0.5 * x)


def _silu(x):
    return x * _sigmoid(x)


def _softplus(x):
    return jnp.maximum(x, 0.0) + jnp.log1p(jnp.exp(-jnp.abs(x)))


def _ada_kernel(c_ref, w_ref, b_ref, o_ref):
    act = _silu(c_ref[...])
    o_ref[...] = _dot(act.astype(BF16), w_ref[...].astype(BF16)) + b_ref[...]


def _ada(c, w_ada, b_ada):
    bsz, d = c.shape
    n = w_ada.shape[1]
    rows = 8
    cp = jnp.zeros((rows, d), F32).at[:bsz].set(c)
    tn = 1536
    out = pl.pallas_call(
        _ada_kernel,
        grid=(n // tn,),
        in_specs=[pl.BlockSpec((rows, d), lambda j: (0, 0)),
                  pl.BlockSpec((d, tn), lambda j: (0, j)),
                  pl.BlockSpec((1, tn), lambda j: (0, j))],
        out_specs=pl.BlockSpec((rows, tn), lambda j: (0, j)),
        out_shape=jax.ShapeDtypeStruct((rows, n), F32),
        compiler_params=pltpu.CompilerParams(vmem_limit_bytes=VMEM_LIMIT),
        name="ada",
    )(cp, w_ada, b_ada.reshape(1, n))
    return out[:bsz]


def _inproj_kernel(x_ref, sh_ref, sc_ref, nw_ref, pos_ref, invf_ref, qn_ref, kn_ref, bd512_ref,
                   rep_ref, cw_ref, wqkv_ref, wz_ref, wab_ref, wsq_ref, wsk_ref, wsv_ref,
                   gq_ref, gk_ref, gv_ref, z_ref, ab_ref, sq_ref, sk_ref, sv_ref, win_ref):
    tm = x_ref.shape[0]

    @pl.when(pl.program_id(1) == 0)
    def _():
        win_ref[:8, :] = jnp.zeros((8, win_ref.shape[1]), F32)

    x = x_ref[...]
    ms = jnp.mean(x * x, axis=-1, keepdims=True)
    h = x * lax.rsqrt(ms + EPS) * nw_ref[...]
    h = h * (1.0 + sc_ref[...]) + sh_ref[...]
    hb = h.astype(BF16)
    sq = _dot(hb, wsq_ref[...])
    sk = _dot(hb, wsk_ref[...])
    sv = _dot(hb, wsv_ref[...])
    win_ref[8:, :] = _dot(hb, wqkv_ref[...])
    z_ref[...] = _dot(hb, wz_ref[...]).astype(BF16)
    ab_ref[...] = _dot(hb, wab_ref[...])
    bd512 = bd512_ref[...]

    cw = cw_ref[...]

    def conv_silu(part):
        cols = slice(part * GDN_WIDTH, (part + 1) * GDN_WIDTH)
        y = None
        for j in range(CONV_K):
            t = win_ref[pl.ds(8 - (CONV_K - 1) + j, tm), cols] * cw[j:j + 1, cols]
            y = t if y is None else y + t
        return _silu(y)

    gq = conv_silu(0)
    gq_ref[...] = (gq * lax.rsqrt(_head_sums(gq * gq, bd512) + EPS)
                   * (HEAD_DIM ** -0.5)).astype(BF16)
    gk = conv_silu(1)
    gk_ref[...] = (gk * lax.rsqrt(_head_sums(gk * gk, bd512) + EPS)).astype(BF16)
    gv_ref[...] = conv_silu(2).astype(BF16)
    win_ref[:8, :] = win_ref[tm:, :]

    ang = pos_ref[...].astype(F32) * invf_ref[...]
    cos1 = jnp.cos(ang)
    sin1 = jnp.sin(ang)
    half = HEAD_DIM // 2

    def norm_rope(t, gain):
        width = t.shape[1]
        reps = width // LANES
        cos = jnp.concatenate([cos1] * reps, axis=1)
        sin = jnp.concatenate([sin1] * reps, axis=1)
        first = (lax.broadcasted_iota(jnp.int32, t.shape, 1) % HEAD_DIM) < half
        ms = _head_sums(t * t, bd512[:width, :width]) * (1.0 / HEAD_DIM)
        t = t * lax.rsqrt(ms + EPS) * gain
        swapped = jnp.where(first, pltpu.roll(t, width - half, 1), pltpu.roll(t, half, 1))
        return t * cos + swapped * jnp.where(first, -sin, sin)

    sq_ref[...] = (norm_rope(sq, qn_ref[...]) * (HEAD_DIM ** -0.5)).astype(BF16)
    kn = kn_ref[...]
    sk_ref[...] = _dot(norm_rope(sk, kn[:, :sk.shape[1]]).astype(BF16), rep_ref[...]).astype(BF16)
    sv_ref[...] = _dot(sv.astype(BF16), rep_ref[...]).astype(BF16)


def _inproj(x, shift, scale, norm_w, positions, q_norm, k_norm, conv_w, wqkv, wz, wab, wsq, wsk, wsv,
            tm):
    bsz, seq, d = x.shape
    half = HEAD_DIM // 2
    inv_freq = jnp.power(jnp.float32(ROPE_THETA), -jnp.arange(half, dtype=F32) / half)
    invf = jnp.tile(inv_freq, LANES // half).reshape(1, LANES)
    qn = jnp.tile(q_norm, SWA_Q_HEADS).reshape(1, SWA_WIDTH)
    kn = jnp.tile(k_norm, SWA_Q_HEADS).reshape(1, SWA_WIDTH)
    h512 = np.arange(SWA_WIDTH) // HEAD_DIM
    bd512 = jnp.asarray((h512[:, None] == h512[None, :]).astype(np.float32), BF16)
    kv_lane = (h512 // SWA_GROUP) * HEAD_DIM + np.arange(SWA_WIDTH) % HEAD_DIM
    rep = jnp.asarray((np.arange(SWA_KV_HEADS * HEAD_DIM)[:, None] == kv_lane[None, :]
                       ).astype(np.float32), BF16)
    row = lambda b, i: (b, i, 0)
    vec = lambda b, i: (b, 0, 0)
    const = lambda b, i: (0, 0)
    widths = (GDN_WIDTH, GDN_WIDTH, GDN_WIDTH, GDN_WIDTH, LANES, SWA_WIDTH, SWA_WIDTH, SWA_WIDTH)
    dtypes = (BF16, BF16, BF16, BF16, F32, BF16, BF16, BF16)
    return pl.pallas_call(
        _inproj_kernel,
        grid=(bsz, seq // tm),
        in_specs=[pl.BlockSpec((None, tm, d), row),
                  pl.BlockSpec((None, 1, d), vec),
                  pl.BlockSpec((None, 1, d), vec),
                  pl.BlockSpec((1, d), const),
                  pl.BlockSpec((None, tm, 1), row)]
                 + [pl.BlockSpec(w.shape, const)
                    for w in (invf, qn, kn, bd512, rep, conv_w, wqkv, wz, wab, wsq, wsk, wsv)],
        out_specs=[pl.BlockSpec((None, tm, w), row) for w in widths],
        out_shape=[jax.ShapeDtypeStruct((bsz, seq, w), t) for w, t in zip(widths, dtypes)],
        scratch_shapes=[pltpu.VMEM((tm + 8, 3 * GDN_WIDTH), F32)],
        compiler_params=pltpu.CompilerParams(
            dimension_semantics=("parallel", "arbitrary"), vmem_limit_bytes=VMEM_LIMIT),
        name="inproj",
    )(x, shift, scale, norm_w, positions.reshape(bsz, seq, 1), invf, qn, kn, bd512, rep, conv_w,
      wqkv, wz, wab, wsq, wsk, wsv)


def _gdn_consts(tb):
    r = np.arange(CHUNK)[:, None]
    cidx = np.arange(TILE)[None, :] % HEAD_DIM
    low = (r >= cidx).astype(np.float32)
    strict = (r > cidx).astype(np.float32)
    eye = (r == cidx).astype(np.float32)
    upper = (r <= cidx).astype(np.float32)
    hb = np.arange(TILE) // HEAD_DIM
    blk = (hb[:, None] == hb[None, :]).astype(np.float32)
    h512 = np.arange(GDN_WIDTH) // HEAD_DIM
    bd512 = (h512[:, None] == h512[None, :]).astype(np.float32)
    e128 = np.zeros((LANES, 2 * GDN_WIDTH), np.float32)
    for j in range(2 * GDN_HEADS):
        e128[j, j * HEAD_DIM:(j + 1) * HEAD_DIM] = 1.0
    t = np.arange(tb)
    lbd = ((t[:, None] >= t[None, :]) & (t[:, None] // CHUNK == t[None, :] // CHUNK))
    return dict(low=jnp.asarray(low), strict=jnp.asarray(strict), eye=jnp.asarray(eye),
                upper=jnp.asarray(upper), blk=jnp.asarray(blk, BF16),
                bd512=jnp.asarray(bd512, BF16), e128=jnp.asarray(e128, BF16),
                lbd=jnp.asarray(lbd.astype(np.float32), BF16))


def _gdn_kernel(q_ref, k_ref, v_ref, z_ref, ab_ref, alog_ref, dt_ref, gamma_ref,
                bd512_ref, e128_ref, lbd_ref, low_ref, strict_ref, eye_ref, upper_ref, blk_ref,
                out_ref,
                s_ref, g_s, b_s, gc_s, o_s, u_s, at_s, kd_s, wq_s, eg_s,
                *, tb, nb):
    step = pl.program_id(0)

    @pl.when(step == 0)
    def _():
        s_ref[...] = jnp.zeros_like(s_ref)

    bd512 = bd512_ref[...]
    low = low_ref[...] > 0.5
    strict = strict_ref[...] > 0.5
    upper = upper_ref[...] > 0.5
    eye = eye_ref[...]
    blk = blk_ref[...]
    blk_f = blk.astype(F32)
    n_tiles = GDN_WIDTH // TILE
    lane_tiles = [slice(g * TILE, (g + 1) * TILE) for g in range(n_tiles)]
    n_chunks = tb // CHUNK

    def prepare(b):
        ab = ab_ref[b]
        lane = lax.broadcasted_iota(jnp.int32, ab.shape, 1)
        gb = jnp.where(lane < GDN_HEADS,
                       -jnp.exp(alog_ref[...]) * _softplus(ab + dt_ref[...]),
                       _sigmoid(ab))
        lbd = lbd_ref[...]
        gcum = None
        for p in _split(gb, 3):
            t = _dot(lbd, p)
            gcum = t if gcum is None else gcum + t
        yield
        e128 = e128_ref[...]
        gbe = _dot_split_lhs(gb, e128, 3)
        g_s[b] = gbe[:, :GDN_WIDTH]
        b_s[b] = gbe[:, GDN_WIDTH:]
        yield
        gc_s[b] = _dot_split_lhs(gcum, e128[:, :GDN_WIDTH], 3)

    def bdiag(t):
        tb16 = t.astype(BF16)
        return jnp.concatenate([tb16] * HEADS_PER_TILE, axis=0) * blk

    def intra_chunk(units, filler):
        n = range(len(units))
        bi = [b for b, _, _ in units]
        rows = [pl.ds(c * CHUNK, CHUNK) for _, c, _ in units]
        sl = [lane_tiles[g] for _, _, g in units]
        kc = [k_ref[bi[i], rows[i], sl[i]].astype(F32) for i in n]
        qc = [q_ref[bi[i], rows[i], sl[i]].astype(F32) for i in n]
        bc = [b_s[bi[i], rows[i], sl[i]] for i in n]
        gcc = [gc_s[bi[i], rows[i], sl[i]] for i in n]
        grow = [jnp.sum(jnp.where(upper, g_s[bi[i], rows[i], sl[i]], 0.0), axis=0, keepdims=True)
                for i in n]
        decay = [jnp.where(low, jnp.exp(jnp.where(low, gcc[i] - grow[i], 0.0)), 0.0) for i in n]
        kb = [kc[i] * bc[i] for i in n]
        mm = [_dot_nt(jnp.concatenate([kb[i], qc[i]], axis=0).astype(BF16), bdiag(kc[i]))
              for i in n]
        next(filler, None)
        m = [jnp.where(strict, mm[i][:CHUNK] * decay[i], 0.0) for i in n]
        for i in n:
            at_s[bi[i], rows[i], sl[i]] = (mm[i][CHUNK:] * decay[i]).astype(BF16)

        pw = [_dot(m[i].astype(BF16), bdiag(m[i])) for i in n]
        tinv = [eye - m[i] for i in n]
        for _ in range(4):
            both = [_dot(jnp.concatenate([pw[i], tinv[i]], axis=0).astype(BF16), bdiag(pw[i]))
                    for i in n]
            next(filler, None)
            pw = [both[i][:CHUNK] for i in n]
            tinv = [tinv[i] + both[i][CHUNK:] for i in n]
        last = [_dot(tinv[i].astype(BF16), bdiag(pw[i])) for i in n]
        next(filler, None)
        tinv16 = [(tinv[i] + last[i]).astype(BF16) for i in n]

        eg = [jnp.exp(gcc[i]) for i in n]
        u = [_dot(tinv16[i], bdiag(v_ref[bi[i], rows[i], sl[i]].astype(F32) * bc[i])) for i in n]
        w = [_dot(tinv16[i], bdiag(kb[i] * eg[i])) for i in n]
        for i, (b, c, g) in enumerate(units):
            glast = gcc[i][CHUNK - 1:CHUNK, :]
            u_s[b, rows[i], sl[i]] = u[i]
            wq_s[b, c, :CHUNK, sl[i]] = w[i].astype(BF16)
            wq_s[b, c, CHUNK:, sl[i]] = (qc[i] * eg[i]).astype(BF16)
            kd_s[b, rows[i], sl[i]] = (kc[i] * jnp.exp(glast - gcc[i])).astype(BF16)
            eg_s[b, pl.ds(c * 8, 1), sl[i]] = jnp.exp(glast)

    def inter_chunk(c):
        chains = [(b, g) for b in range(nb) for g in range(n_tiles)]
        n = range(len(chains))
        rows = pl.ds(c * CHUNK, CHUNK)
        sl = [lane_tiles[g] for _, g in chains]
        bi = [b for b, _ in chains]
        state = [s_ref[b, g] for b, g in chains]
        ws_qs = [_dot(wq_s[bi[i], c, :, sl[i]], state[i].astype(BF16)) for i in n]
        vnew = [u_s[bi[i], rows, sl[i]] - ws_qs[i][:CHUNK] for i in n]
        intra = [_dot(at_s[bi[i], rows, sl[i]], bdiag(vnew[i])) for i in n]
        kv = [_dot_tn(kd_s[bi[i], rows, sl[i]], vnew[i].astype(BF16)) for i in n]
        for i, (b, g) in enumerate(chains):
            o_s[b, rows, sl[i]] = ws_qs[i][CHUNK:] + intra[i]
            s_ref[b, g] = state[i] * eg_s[b, pl.ds(c * 8, 1), sl[i]] + kv[i] * blk_f

    def finish(b):
        o = o_s[b]
        ms = _head_sums(o * o, bd512) * (1.0 / HEAD_DIM)
        zf = z_ref[b].astype(F32)
        out_ref[b] = (o * lax.rsqrt(ms + EPS) * gamma_ref[...] * _silu(zf)).astype(BF16)

    def drain(gen):
        for _ in gen:
            pass

    drain(prepare(0))
    for b in range(nb):
        filler = prepare(b + 1) if b + 1 < nb else iter(())
        for c0 in range(0, n_chunks, INTRA_CHUNKS):
            intra_chunk([(b, c, g) for c in range(c0, min(c0 + INTRA_CHUNKS, n_chunks))
                         for g in range(n_tiles)], filler)
        drain(filler)
    for c in range(n_chunks):
        inter_chunk(c)
    for b in range(nb):
        finish(b)


def _gdn(q, k, v, z, ab, a_log, dt_bias, out_norm, tb):
    nb, seq, _ = q.shape
    cs = _gdn_consts(tb)
    pad = lambda v: jnp.zeros((1, LANES), F32).at[0, :GDN_HEADS].set(v)
    gamma = jnp.tile(out_norm, GDN_HEADS).reshape(1, GDN_WIDTH)
    row = lambda i: (0, i, 0)
    const = lambda i: (0, 0)
    consts = (cs["bd512"], cs["e128"], cs["lbd"], cs["low"], cs["strict"], cs["eye"],
              cs["upper"], cs["blk"])
    small = (pad(a_log), pad(dt_bias), gamma)
    wide = pltpu.VMEM((nb, tb, GDN_WIDTH), F32)
    wide16 = pltpu.VMEM((nb, tb, GDN_WIDTH), BF16)
    n_chunks = tb // CHUNK
    return pl.pallas_call(
        functools.partial(_gdn_kernel, tb=tb, nb=nb),
        grid=(seq // tb,),
        in_specs=[pl.BlockSpec((nb, tb, GDN_WIDTH), row)] * 4
                 + [pl.BlockSpec((nb, tb, LANES), row)]
                 + [pl.BlockSpec(a.shape, const) for a in small + consts],
        out_specs=pl.BlockSpec((nb, tb, GDN_WIDTH), row),
        out_shape=jax.ShapeDtypeStruct((nb, seq, GDN_WIDTH), BF16),
        scratch_shapes=[pltpu.VMEM((nb, GDN_WIDTH // TILE, TILE, TILE), F32),
                        wide, wide, wide, wide, wide,
                        wide16, wide16,
                        pltpu.VMEM((nb, n_chunks, 2 * CHUNK, GDN_WIDTH), BF16),
                        pltpu.VMEM((nb, n_chunks * 8, GDN_WIDTH), F32)],
        compiler_params=pltpu.CompilerParams(
            dimension_semantics=("arbitrary",), vmem_limit_bytes=VMEM_LIMIT),
        name="gdn",
    )(q, k, v, z, ab, *small, *consts)


def _swa_kernel(q_ref, k0_ref, k1_ref, k2_ref, v0_ref, v1_ref, v2_ref, sink_ref, valid_ref,
                hmask_ref, ones_ref, out_ref):
    step = pl.program_id(1)
    hmask = hmask_ref[...]
    heads = range(SWA_GROUP)
    k_blocks = (k0_ref, k1_ref, k2_ref)
    v_blocks = (v0_ref, v1_ref, v2_ref)
    in_window = valid_ref[...] > 0.5
    valid = (in_window & ((step > 0) | (valid_ref[...] > 1.5)), in_window)
    units = [(j, g) for j in range(SWA_QBLOCKS) for g in range(SWA_KV_HEADS)]
    n = range(len(units))
    rows = [slice(j * BLOCK, (j + 1) * BLOCK) for j, _ in units]
    sl = [slice(g * TILE, (g + 1) * TILE) for _, g in units]
    kband = [jnp.concatenate([k_blocks[j][:, sl[i]], k_blocks[j + 1][:, sl[i]]], axis=0)
             for i, (j, _) in enumerate(units)]
    vband = [jnp.concatenate([v_blocks[j][:, sl[i]], v_blocks[j + 1][:, sl[i]]], axis=0)
             for i, (j, _) in enumerate(units)]
    lhs = [jnp.concatenate([q_ref[rows[i], sl[i]].astype(F32) * hmask[h:h + 1] for h in heads],
                           axis=0).astype(BF16) for i in n]
    s = [jnp.where(valid[units[i][0]], _dot_nt(lhs[i], kband[i]), -jnp.inf) for i in n]
    sink = [sink_ref[g] for _, g in units]
    mx = [jnp.maximum(jnp.max(s[i], axis=1, keepdims=True), sink[i]) for i in n]
    p = [jnp.exp(s[i] - mx[i]).astype(BF16) for i in n]
    psum = [_dot(p[i], ones_ref[...]) for i in n]
    pv = [_dot(p[i], vband[i]) for i in n]
    for i in n:
        rinv = 1.0 / (psum[i] + jnp.exp(sink[i] - mx[i]))
        out = None
        for h in heads:
            hrows = slice(h * BLOCK, (h + 1) * BLOCK)
            scale = jnp.concatenate([rinv[hrows]] * (TILE // LANES), axis=1) * hmask[h:h + 1]
            term = pv[i][hrows] * scale
            out = term if out is None else out + term
        out_ref[rows[i], sl[i]] = out.astype(BF16)


def _swa(sq, sk, sv, sinks):
    bsz, seq, _ = sq.shape
    sink_rows = jnp.repeat(sinks.astype(F32), BLOCK).reshape(SWA_KV_HEADS, SWA_GROUP * BLOCK, 1)
    qi = np.arange(BLOCK)[:, None]
    kj = np.arange(2 * BLOCK)[None, :]
    rel = qi + BLOCK - kj
    inwin = ((rel >= 0) & (rel < WINDOW)).astype(np.float32)
    valid = inwin * np.where(kj >= BLOCK, 2.0, 1.0)
    valid = jnp.asarray(np.tile(valid, (SWA_GROUP, 1)), F32)
    hm = (np.arange(TILE)[None, :] // HEAD_DIM == np.arange(SWA_GROUP)[:, None]).astype(np.float32)
    const2 = lambda b, i: (0, 0)
    const3 = lambda b, i: (0, 0, 0)
    blk = (None, BLOCK, SWA_WIDTH)
    key_blocks = [pl.BlockSpec(blk, lambda b, i: (b, jnp.maximum(SWA_QBLOCKS * i - 1, 0), 0)),
                  pl.BlockSpec(blk, lambda b, i: (b, SWA_QBLOCKS * i, 0)),
                  pl.BlockSpec(blk, lambda b, i: (b, SWA_QBLOCKS * i + 1, 0))]
    qrows = (None, SWA_QBLOCKS * BLOCK, SWA_WIDTH)
    return pl.pallas_call(
        _swa_kernel,
        grid=(bsz, seq // (SWA_QBLOCKS * BLOCK)),
        in_specs=[pl.BlockSpec(qrows, lambda b, i: (b, i, 0))] + key_blocks + key_blocks
                 + [pl.BlockSpec(sink_rows.shape, const3),
                    pl.BlockSpec(valid.shape, const2),
                    pl.BlockSpec(hm.shape, const2),
                    pl.BlockSpec((2 * BLOCK, LANES), const2)],
        out_specs=pl.BlockSpec(qrows, lambda b, i: (b, i, 0)),
        out_shape=jax.ShapeDtypeStruct((bsz, seq, SWA_WIDTH), BF16),
        compiler_params=pltpu.CompilerParams(
            dimension_semantics=("parallel", "parallel"), vmem_limit_bytes=VMEM_LIMIT),
        name="swa",
    )(sq, sk, sk, sk, sv, sv, sv, sink_rows, valid, jnp.asarray(hm),
      jnp.ones((2 * BLOCK, LANES), BF16))


def _outproj_kernel(gdn_ref, swa_ref, x_ref, gate_ref, sh_ref, sc_ref, nw_ref, wo1_ref, wo2_ref,
                    wgrp_ref, bgrp_ref, wrt_ref, brt_ref, x1_ref, h2_ref, comb_ref):
    mixed = _dot(gdn_ref[...], wo1_ref[...]) + _dot(swa_ref[...], wo2_ref[...])
    x1 = x_ref[...] + gate_ref[...] * mixed
    x1_ref[...] = x1
    ms = jnp.mean(x1 * x1, axis=-1, keepdims=True)
    h2 = x1 * lax.rsqrt(ms + EPS) * nw_ref[...]
    h2 = h2 * (1.0 + sc_ref[...]) + sh_ref[...]
    h2b = h2.astype(BF16)
    h2_ref[...] = h2

    lg = _dot(h2b, wgrp_ref[...]) + bgrp_ref[...]
    lane = lax.broadcasted_iota(jnp.int32, lg.shape, 1).astype(F32)
    none = jnp.float32(LANES)
    lg = jnp.where(lane < N_GROUPS, lg, -jnp.inf)
    gmax = jnp.max(lg, axis=1, keepdims=True)
    gidx = jnp.min(jnp.where(lg == gmax, lane, none), axis=1, keepdims=True)
    p_group = 1.0 / jnp.sum(jnp.exp(lg - gmax), axis=1, keepdims=True)
    le = _dot(h2b, wrt_ref[...]) + brt_ref[...]
    lo = gidx * EXPERTS_PER_GROUP
    in_group = (lane >= lo) & (lane < lo + EXPERTS_PER_GROUP)
    le = jnp.where(in_group, le, -jnp.inf)
    pe = jnp.exp(le - jnp.max(le, axis=1, keepdims=True))
    pe = pe / jnp.sum(pe, axis=1, keepdims=True)
    p1 = jnp.max(pe, axis=1, keepdims=True)
    i1 = jnp.min(jnp.where(in_group & (pe == p1), lane, none), axis=1, keepdims=True)
    rest = in_group & (lane != i1)
    pe2 = jnp.where(rest, pe, -1.0)
    p2 = jnp.max(pe2, axis=1, keepdims=True)
    i2 = jnp.min(jnp.where(rest & (pe2 == p2), lane, none), axis=1, keepdims=True)
    wsum = p1 + p2
    comb_ref[...] = (jnp.where(lane == i1, 1.0, 0.0)
                     + jnp.where(lane == i2 + N_EXPERTS, 1.0, 0.0)
                     + jnp.where(lane == W1_LANE, p1 / wsum * p_group, 0.0)
                     + jnp.where(lane == W1_LANE + 1, p2 / wsum * p_group, 0.0))


def _outproj(gdn, swa, x, gate, shift, scale, norm_w, wo1, wo2, wgrp, bgrp, wrt, brt, tm):
    bsz, seq, d = x.shape
    row = lambda b, i: (b, i, 0)
    vec = lambda b, i: (b, 0, 0)
    const = lambda b, i: (0, 0)
    return pl.pallas_call(
        _outproj_kernel,
        grid=(bsz, seq // tm),
        in_specs=[pl.BlockSpec((None, tm, GDN_WIDTH), row),
                  pl.BlockSpec((None, tm, SWA_WIDTH), row),
                  pl.BlockSpec((None, tm, d), row),
                  pl.BlockSpec((None, 1, d), vec),
                  pl.BlockSpec((None, 1, d), vec),
                  pl.BlockSpec((None, 1, d), vec),
                  pl.BlockSpec((1, d), const)]
                 + [pl.BlockSpec(a.shape, const) for a in (wo1, wo2, wgrp, bgrp, wrt, brt)],
        out_specs=[pl.BlockSpec((None, tm, d), row),
                   pl.BlockSpec((None, tm, d), row),
                   pl.BlockSpec((None, tm, LANES), row)],
        out_shape=[jax.ShapeDtypeStruct((bsz, seq, d), F32),
                   jax.ShapeDtypeStruct((bsz, seq, d), F32),
                   jax.ShapeDtypeStruct((bsz, seq, LANES), F32)],
        compiler_params=pltpu.CompilerParams(
            dimension_semantics=("parallel", "parallel"), vmem_limit_bytes=VMEM_LIMIT),
        name="outproj",
    )(gdn, swa, x, gate, shift, scale, norm_w, wo1, wo2, wgrp, bgrp, wrt, brt)


def _route_kernel(info_ref, lstrict_ref, ustrict_ref, sel_ref, pos_ref, cnt_ref,
                  total_ref, run_ref, off_ref, *, tile):
    phase = pl.program_id(0)
    step = pl.program_id(1)
    info = info_ref[...]
    lane = lax.broadcasted_iota(jnp.int32, info.shape, 1)
    first = jnp.where(lane < N_EXPERTS, info, 0.0)
    second = pltpu.roll(jnp.where((lane >= N_EXPERTS) & (lane < 2 * N_EXPERTS), info, 0.0),
                        LANES - N_EXPERTS, 1)
    ind = first + second
    colsum = jnp.sum(ind, axis=0, keepdims=True)

    @pl.when((phase == 0) & (step == 0))
    def _():
        total_ref[...] = jnp.zeros_like(total_ref)

    @pl.when(phase == 0)
    def _():
        total_ref[...] += colsum

    @pl.when((phase == 1) & (step == 0))
    def _():
        cap = jnp.ceil(total_ref[...] * (1.0 / tile)) * tile
        cap8 = jnp.broadcast_to(cap, (8, LANES))
        off_ref[...] = _dot_split_lhs(cap8, ustrict_ref[...], 3)[0:1]
        run_ref[...] = jnp.zeros_like(run_ref)

    @pl.when(phase == 1)
    def _():
        base = _dot(lstrict_ref[...], ind.astype(BF16)) + run_ref[...] + off_ref[...]
        pos1 = jnp.sum(first * base, axis=1, keepdims=True)
        pos2 = jnp.sum(second * base, axis=1, keepdims=True)
        run_ref[...] += colsum
        both = jnp.where(lane == 0, pos1, 0.0) + jnp.where(lane == 1, pos2, 0.0)
        rows = None
        for p in _split(both, 3):
            t = _dot_nt(sel_ref[...], p)
            rows = t if rows is None else rows + t
        pos_ref[...] = rows.astype(jnp.int32)

    cnt_ref[...] = jnp.broadcast_to(total_ref[...], cnt_ref.shape)


def _route(info, tt, tile):
    n_tok = info.shape[0]
    r = np.arange(tt)
    lstrict = jnp.asarray((r[:, None] > r[None, :]).astype(np.float32), BF16)
    l = np.arange(LANES)
    ustrict = jnp.asarray((l[:, None] < l[None, :]).astype(np.float32), BF16)
    sel = jnp.asarray(np.eye(8, LANES, dtype=np.float32), BF16)
    const = lambda p, i: (0, 0)
    return pl.pallas_call(
        functools.partial(_route_kernel, tile=tile),
        grid=(2, n_tok // tt),
        in_specs=[pl.BlockSpec((tt, LANES), lambda p, i: (i, 0)),
                  pl.BlockSpec((tt, tt), const),
                  pl.BlockSpec((LANES, LANES), const),
                  pl.BlockSpec((8, LANES), const)],
        out_specs=[pl.BlockSpec((8, tt), lambda p, i: (0, i * p)),
                   pl.BlockSpec((8, LANES), const)],
        out_shape=[jax.ShapeDtypeStruct((8, n_tok), jnp.int32),
                   jax.ShapeDtypeStruct((8, LANES), F32)],
        scratch_shapes=[pltpu.VMEM((1, LANES), F32)] * 3,
        compiler_params=pltpu.CompilerParams(
            dimension_semantics=("arbitrary", "arbitrary"), vmem_limit_bytes=VMEM_LIMIT),
        name="route",
    )(info, lstrict, ustrict, sel)


def _row_copy(src, src_row, dst, dst_row, sem):
    return pltpu.make_async_copy(src.at[pl.ds(src_row, 1)], dst.at[pl.ds(dst_row, 1)], sem)


def _for_each_row(lo, hi, fn):
    def trip(t, carry):
        for j in range(ROW_UNROLL):
            fn(t * ROW_UNROLL + j)
        return carry

    lax.fori_loop(lo // ROW_UNROLL, hi // ROW_UNROLL, trip, 0)


def _dispatch_kernel(cnt_ref, off_ref, cap_ref, nu_ref, pos1_ref, pos2_ref, h2_ref, xs_ref,
                     zero_ref, sem):
    step = pl.program_id(0)
    tm = h2_ref.shape[0]
    tile = zero_ref.shape[0]

    @pl.when(step == 0)
    def _():
        zero_ref[...] = jnp.zeros_like(zero_ref)

        def tail_copy(t):
            return pltpu.make_async_copy(zero_ref, xs_ref.at[pl.ds(t * tile, tile)], sem)

        def tail_start(t, c):
            tail_copy(t).start()
            return c

        def tail_wait(t, c):
            tail_copy(t).wait()
            return c

        lax.fori_loop(nu_ref[0], xs_ref.shape[0] // tile, tail_start, 0)
        lax.fori_loop(nu_ref[0], xs_ref.shape[0] // tile, tail_wait, 0)

        def per_expert(e, carry):
            lo = off_ref[e] + cnt_ref[e]
            hi = off_ref[e] + cap_ref[e]

            def start(s, c):
                _row_copy(zero_ref, 0, xs_ref, s, sem).start()
                return c

            def wait(s, c):
                _row_copy(zero_ref, 0, xs_ref, s, sem).wait()
                return c

            lax.fori_loop(lo, hi, start, 0)
            lax.fori_loop(lo, hi, wait, 0)
            return carry

        lax.fori_loop(0, N_EXPERTS, per_expert, 0)

    def copies(r):
        return (_row_copy(h2_ref, r, xs_ref, pos1_ref[r], sem),
                _row_copy(h2_ref, r, xs_ref, pos2_ref[r], sem))

    _for_each_row(0, tm, lambda r: [cp.start(priority=p) for p, cp in enumerate(copies(r))])
    _for_each_row(0, tm, lambda r: [cp.wait() for cp in copies(r)])


def _dispatch(h2, pos1, pos2, cnt, off, cap, n_used, n_rows, tile, tm):
    n_tok, d = h2.shape
    smem_tok = pl.BlockSpec((tm,), lambda i, *_: (i,), memory_space=pltpu.SMEM)
    return pl.pallas_call(
        _dispatch_kernel,
        grid_spec=pltpu.PrefetchScalarGridSpec(
            num_scalar_prefetch=4,
            grid=(n_tok // tm,),
            in_specs=[smem_tok, smem_tok, pl.BlockSpec((tm, d), lambda i, *_: (i, 0))],
            out_specs=pl.BlockSpec(memory_space=pl.ANY),
            scratch_shapes=[pltpu.VMEM((tile, d), F32), pltpu.SemaphoreType.DMA]),
        out_shape=jax.ShapeDtypeStruct((n_rows, d), F32),
        compiler_params=pltpu.CompilerParams(
            dimension_semantics=("arbitrary",), vmem_limit_bytes=VMEM_LIMIT),
        name="dispatch",
    )(cnt, off, cap, n_used, pos1, pos2, h2)


def _gmm_kernel(te_ref, nu_ref, xs_ref, wg_ref, wu_ref, wd_ref, ys_ref, wgu16, wd16):
    step = pl.program_id(0)
    de = wg_ref.shape[1]

    @pl.when((step == 0) | (te_ref[step] != te_ref[jnp.maximum(step - 1, 0)]))
    def _():
        wgu16[:, :de] = wg_ref[...].astype(BF16)
        wgu16[:, de:] = wu_ref[...].astype(BF16)
        wd16[...] = wd_ref[...].astype(BF16)

    @pl.when(step < nu_ref[0])
    def _():
        x = xs_ref[...].astype(BF16)
        gu = _dot(x, wgu16[...])
        hid = _silu(gu[:, :de]) * gu[:, de:]
        ys_ref[...] = _dot(hid.astype(BF16), wd16[...])

    @pl.when(step >= nu_ref[0])
    def _():
        ys_ref[...] = jnp.zeros_like(ys_ref)


def _gmm(xs, tile_expert, n_used, wg, wu, wd, tile):
    n_rows, d = xs.shape
    de = wg.shape[-1]
    wsel = lambda i, te, nu: (te[i], 0, 0)
    return pl.pallas_call(
        _gmm_kernel,
        grid_spec=pltpu.PrefetchScalarGridSpec(
            num_scalar_prefetch=2,
            grid=(n_rows // tile,),
            in_specs=[pl.BlockSpec((tile, d), lambda i, te, nu: (jnp.where(i < nu[0], i, 0), 0)),
                      pl.BlockSpec((None, d, de), wsel),
                      pl.BlockSpec((None, d, de), wsel),
                      pl.BlockSpec((None, de, d), wsel)],
            out_specs=pl.BlockSpec((tile, d), lambda i, te, nu: (i, 0)),
            scratch_shapes=[pltpu.VMEM((d, 2 * de), BF16), pltpu.VMEM((de, d), BF16)]),
        out_shape=jax.ShapeDtypeStruct((n_rows, d), F32),
        compiler_params=pltpu.CompilerParams(
            dimension_semantics=("arbitrary",), vmem_limit_bytes=VMEM_LIMIT),
        name="gmm",
    )(tile_expert, n_used, xs, wg, wu, wd)


def _combine_kernel(pos1_ref, pos2_ref, x1_ref, info_ref, gate_ref, ys_ref, out_ref, ybuf, sem):
    tm = x1_ref.shape[0]

    half = tm // 2

    def copies(r, part):
        return (_row_copy(ys_ref, pos1_ref[r], ybuf.at[0], r, sem.at[part]),
                _row_copy(ys_ref, pos2_ref[r], ybuf.at[1], r, sem.at[part]))

    def start(r, part):
        for p, cp in enumerate(copies(r, part)):
            cp.start(priority=p)

    for part, lo in enumerate((0, half)):
        _for_each_row(lo, lo + half, functools.partial(start, part=part))
    for part, lo in enumerate((0, half)):
        rows = pl.ds(lo, half)
        _for_each_row(lo, lo + half, lambda r: [cp.wait() for cp in copies(r, part)])
        info = info_ref[rows, :]
        lane = lax.broadcasted_iota(jnp.int32, info.shape, 1)
        w1 = jnp.sum(jnp.where(lane == W1_LANE, info, 0.0), axis=1, keepdims=True)
        w2 = jnp.sum(jnp.where(lane == W1_LANE + 1, info, 0.0), axis=1, keepdims=True)
        out_ref[rows, :] = x1_ref[rows, :] + gate_ref[...] * (w1 * ybuf[0, rows, :]
                                                              + w2 * ybuf[1, rows, :])


def _combine(ys, pos1, pos2, x1, info, gate, seq, tm):
    n_tok, d = x1.shape
    smem_tok = pl.BlockSpec((tm,), lambda i: (i,), memory_space=pltpu.SMEM)
    return pl.pallas_call(
        _combine_kernel,
        grid=(n_tok // tm,),
        in_specs=[smem_tok, smem_tok,
                  pl.BlockSpec((tm, d), lambda i: (i, 0)),
                  pl.BlockSpec((tm, LANES), lambda i: (i, 0)),
                  pl.BlockSpec((None, 1, d), lambda i: (i * tm // seq, 0, 0)),
                  pl.BlockSpec(memory_space=pl.ANY)],
        out_specs=pl.BlockSpec((tm, d), lambda i: (i, 0)),
        out_shape=jax.ShapeDtypeStruct((n_tok, d), F32),
        scratch_shapes=[pltpu.VMEM((2, tm, d), F32), pltpu.SemaphoreType.DMA((2,))],
        compiler_params=pltpu.CompilerParams(
            dimension_semantics=("arbitrary",), vmem_limit_bytes=VMEM_LIMIT),
        name="combine",
    )(pos1, pos2, x1, info, gate, ys)


def _moe(h2, info, x1, gate, wg, wu, wd):
    bsz, seq, d = x1.shape
    n_tok = bsz * seq
    tile = MOE_TILE
    n_rows = 2 * n_tok + N_EXPERTS * tile
    info = info.reshape(n_tok, LANES)
    pos, cnt = _route(info, min(1024, n_tok), tile)
    cnt = cnt[0, :N_EXPERTS].astype(jnp.int32)
    cap = (cnt + tile - 1) // tile * tile
    end = jnp.cumsum(cap)
    off = end - cap
    tile_start = jnp.arange(n_rows // tile, dtype=jnp.int32) * tile
    tile_expert = jnp.minimum(
        jnp.sum((end[None, :] <= tile_start[:, None]).astype(jnp.int32), axis=1), N_EXPERTS - 1)
    n_used = (end[-1:] // tile).astype(jnp.int32)
    tm = min(1024, seq)
    xs = _dispatch(h2.reshape(n_tok, d), pos[0], pos[1], cnt, off, cap, n_used, n_rows, tile, tm)
    ys = _gmm(xs, tile_expert, n_used, wg, wu, wd, tile)
    out = _combine(ys, pos[0], pos[1], x1.reshape(n_tok, d), info, gate, seq, tm)
    return out.reshape(bsz, seq, d)


def _layer(x, c_mod, positions, norm_mix, w_in, conv_w, a_log, dt_bias, gdn_out_norm, q_norm,
           k_norm, sinks, w_out, norm_ffn, w_group, b_group, w_router, b_router, w_gate, w_up,
           w_down):
    bsz, seq, d = x.shape
    shift1, scale1, gate1, shift2, scale2, gate2 = [
        m.reshape(bsz, 1, d) for m in jnp.split(c_mod, 6, axis=-1)]

    o = 0
    w_qkv = w_in[:, o:o + 3 * GDN_WIDTH]; o += 3 * GDN_WIDTH
    w_z = w_in[:, o:o + GDN_WIDTH]; o += GDN_WIDTH
    w_ab = w_in[:, o:o + 2 * GDN_HEADS]; o += 2 * GDN_HEADS
    w_sq = w_in[:, o:o + SWA_WIDTH]; o += SWA_WIDTH
    kvw = SWA_KV_HEADS * HEAD_DIM
    w_sk = w_in[:, o:o + kvw]; o += kvw
    w_sv = w_in[:, o:o + kvw]
    w_ab = jnp.zeros((d, LANES), F32).at[:, :2 * GDN_HEADS].set(w_ab)
    tm = min(512, seq)
    gq, gk, gv, z, ab, sq, sk, sv = _inproj(
        x, shift1, scale1, norm_mix.reshape(1, d), positions, q_norm, k_norm, conv_w,
        w_qkv.astype(BF16), w_z.astype(BF16),
        w_ab.astype(BF16), w_sq.astype(BF16), w_sk.astype(BF16), w_sv.astype(BF16), tm)

    gdn = _gdn(gq, gk, gv, z, ab, a_log, dt_bias, gdn_out_norm, min(256, seq))
    swa = _swa(sq, sk, sv, sinks)

    padw = lambda w: jnp.zeros((d, LANES), F32).at[:, :w.shape[1]].set(w).astype(BF16)
    padb = lambda b: jnp.zeros((1, LANES), F32).at[0, :b.shape[0]].set(b)
    x1, h2, comb = _outproj(
        gdn, swa, x, gate1, shift2, scale2, norm_ffn.reshape(1, d),
        w_out[:GDN_WIDTH].astype(BF16), w_out[GDN_WIDTH:].astype(BF16),
        padw(w_group), padb(b_group), padw(w_router), padb(b_router), tm)

    return _moe(h2, comb, x1, gate2, w_gate, w_up, w_down)


def kernel(x, c, positions, w_ada, b_ada, norm_mix, w_in, conv_w, a_log, dt_bias, gdn_out_norm,
           q_norm, k_norm, sinks, w_out, norm_ffn, w_group, b_group, w_router, b_router, w_gate,
           w_up, w_down):
    depth = w_ada.shape[0]
    for l in range(depth):
        c_mod = _ada(c, w_ada[l], b_ada[l])
        x = _layer(x, c_mod, positions, norm_mix[l], w_in[l], conv_w[l], a_log[l], dt_bias[l],
                   gdn_out_norm[l], q_norm[l], k_norm[l], sinks[l], w_out[l], norm_ffn[l],
                   w_group[l], b_group[l], w_router[l], b_router[l], w_gate[l], w_up[l],
                   w_down[l])
    return x
```
